```python
import math
import jax, jax.numpy as jnp
from jax import lax
import numpy as np

D_MODEL = 1024
BATCH = 2
SEQ = 8192
DEPTH = 2

N_HEADS_A = 8
D_COMP_A = 32
D_V_A = 2 * D_COMP_A
WIDTH_A = N_HEADS_A * D_V_A
ROPE_DIMS = D_COMP_A // 4
ROPE_THETA = 500000.0
Q_BLOCK = 128
WIDTH_B = 512
CONV_WIDTH = 31
N_GROUPS_C = 4
GROUP_C = 128
WIDTH_C = N_GROUPS_C * GROUP_C
N_BRANCH = 3
IN_WIDTH = 3 * WIDTH_A + 2 * WIDTH_B + WIDTH_C
D_FF = -(-8 * D_MODEL // (3 * 256)) * 256
EPS = 1e-6

kernel_name = 'hybrid_diffattn_conformer_fnet_gated'


def rmsnorm(x, g):
    xf = x.astype(jnp.float32)
    y = xf * lax.rsqrt(jnp.mean(xf * xf, axis=-1, keepdims=True) + EPS)
    return (y * g.astype(jnp.float32)).astype(x.dtype)


def layernorm(x, g, b):
    xf = x.astype(jnp.float32)
    mu = jnp.mean(xf, axis=-1, keepdims=True)
    var = jnp.mean(jnp.square(xf - mu), axis=-1, keepdims=True)
    y = (xf - mu) * lax.rsqrt(var + EPS)
    return (y * g.astype(jnp.float32) + b.astype(jnp.float32)).astype(x.dtype)


def rope_tables(seq):
    pos = jnp.arange(seq, dtype=jnp.float32)
    inv = 1.0 / (ROPE_THETA ** (jnp.arange(0, ROPE_DIMS, 2, dtype=jnp.float32) / ROPE_DIMS))
    ang = pos[:, None] * inv[None, :]
    return jnp.cos(ang), jnp.sin(ang)


def apply_partial_rope(t, cos, sin):
    half = ROPE_DIMS // 2
    c = cos[None, :, None, None, :].astype(t.dtype)
    s = sin[None, :, None, None, :].astype(t.dtype)
    r1 = t[..., :half]
    r2 = t[..., half:ROPE_DIMS]
    rest = t[..., ROPE_DIMS:]
    return jnp.concatenate([r1 * c - r2 * s, r2 * c + r1 * s, rest], axis=-1)


def diff_attention(q, k, v, lam):
    b, s = q.shape[0], q.shape[1]
    nb = s // Q_BLOCK
    qb = q.reshape(b, nb, Q_BLOCK, N_HEADS_A, 2, D_COMP_A).transpose(1, 0, 2, 3, 4, 5)
    vf = v.astype(jnp.float32)
    scale = D_COMP_A ** -0.5

    def block(qi):
        sc = jnp.einsum('bqhcd,bkhcd->bhcqk', qi, k,
                        preferred_element_type=jnp.float32) * scale
        p = jax.nn.softmax(sc, axis=-1)
        w = p[:, :, 0] - lam * p[:, :, 1]
        return jnp.einsum('bhqk,bkhe->bqhe', w, vf)

    out = lax.map(block, qb)
    return out.transpose(1, 0, 2, 3, 4).reshape(b, s, N_HEADS_A, D_V_A).astype(v.dtype)


def setup_inputs(seed: int = 0) -> dict:
    key = jax.random.key(seed)
    ks = jax.random.split(key, 24)
    f32 = jnp.float32
    L, D = DEPTH, D_MODEL

    def nrm(k, shape, scale):
        return jax.random.normal(k, shape, f32) * scale

    return {
        'x': nrm(ks[0], (BATCH, SEQ, D), 1.0),
        'norm1_g': 1.0 + nrm(ks[1], (L, D), 0.02),
        'w_in': nrm(ks[2], (L, D, IN_WIDTH), D ** -0.5),
        'qnorm_g': 1.0 + nrm(ks[3], (L, D_COMP_A), 0.02),
        'knorm_g': 1.0 + nrm(ks[4], (L, D_COMP_A), 0.02),
        'lambda_q1': nrm(ks[5], (L, D_COMP_A), 0.1),
        'lambda_k1': nrm(ks[6], (L, D_COMP_A), 0.1),
        'lambda_q2': nrm(ks[7], (L, D_COMP_A), 0.1),
        'lambda_k2': nrm(ks[8], (L, D_COMP_A), 0.1),
        'subln_g': 1.0 + nrm(ks[9], (L, D_V_A), 0.02),
        'w_proj_a': nrm(ks[10], (L, WIDTH_A, D), WIDTH_A ** -0.5),
        'conv_w': nrm(ks[11], (L, CONV_WIDTH, WIDTH_B), CONV_WIDTH ** -0.5),
        'conv_b': nrm(ks[12], (L, WIDTH_B), 0.02),
        'conv_ln_g': 1.0 + nrm(ks[13], (L, WIDTH_B), 0.02),
        'conv_ln_b': nrm(ks[14], (L, WIDTH_B), 0.02),
        'w_proj_b': nrm(ks[15], (L, WIDTH_B, D), WIDTH_B ** -0.5),
        'w_proj_c': nrm(ks[16], (L, WIDTH_C, D), WIDTH_C ** -0.5),
        'w_gate': nrm(ks[17], (L, D, N_BRANCH * D), D ** -0.5),
        'b_gate': nrm(ks[18], (L, N_BRANCH * D), 0.02),
        'w_out': nrm(ks[19], (L, D, D), D ** -0.5),
        'norm2_g': 1.0 + nrm(ks[20], (L, D), 0.02),
        'w_ffn_in': nrm(ks[21], (L, D, 2 * D_FF), D ** -0.5),
        'w_ffn_out': nrm(ks[22], (L, D_FF, D), D_FF ** -0.5),
    }


def reference(x, norm1_g, w_in, qnorm_g, knorm_g, lambda_q1, lambda_k1, lambda_q2, lambda_k2,
              subln_g, w_proj_a, conv_w, conv_b, conv_ln_g, conv_ln_b, w_proj_b, w_proj_c,
              w_gate, b_gate, w_out, norm2_g, w_ffn_in, w_ffn_out):
    b, s, d = x.shape
    cos, sin = rope_tables(s)
    for l in range(DEPTH):
        h = rmsnorm(x, norm1_g[l])
        u = h @ w_in[l]
        o = 0
        q = u[..., o:o + WIDTH_A].reshape(b, s, N_HEADS_A, 2, D_COMP_A); o += WIDTH_A
        k = u[..., o:o + WIDTH_A].reshape(b, s, N_HEADS_A, 2, D_COMP_A); o += WIDTH_A
        v = u[..., o:o + WIDTH_A].reshape(b, s, N_HEADS_A, D_V_A); o += WIDTH_A
        glu_in = u[..., o:o + 2 * WIDTH_B]; o += 2 * WIDTH_B
        four_in = u[..., o:o + WIDTH_C]

        q = apply_partial_rope(rmsnorm(q, qnorm_g[l]), cos, sin)
        k = apply_partial_rope(rmsnorm(k, knorm_g[l]), cos, sin)
        lam_init = 0.8 - 0.6 * math.exp(-0.3 * l)
        lam = (jnp.exp(jnp.sum(lambda_q1[l].astype(jnp.float32) * lambda_k1[l].astype(jnp.float32)))
               - jnp.exp(jnp.sum(lambda_q2[l].astype(jnp.float32) * lambda_k2[l].astype(jnp.float32)))
               + lam_init)
        att = diff_attention(q, k, v, lam)
        att = rmsnorm(att, subln_g[l]) * (1.0 - lam_init)
        y_a = att.reshape(b, s, WIDTH_A) @ w_proj_a[l]

        ga, gb = jnp.split(glu_in, 2, axis=-1)
        g = ga * jax.nn.sigmoid(gb)
        cv = lax.conv_general_dilated(
            g, conv_w[l].reshape(CONV_WIDTH, 1, WIDTH_B), window_strides=(1,),
            padding=[((CONV_WIDTH - 1) // 2, (CONV_WIDTH - 1) // 2)],
            dimension_numbers=('NWC', 'WIO', 'NWC'), feature_group_count=WIDTH_B)
        cv = layernorm(cv + conv_b[l], conv_ln_g[l], conv_ln_b[l])
        y_b = jax.nn.silu(cv) @ w_proj_b[l]

        fc = four_in.astype(jnp.float32).reshape(b, s, N_GROUPS_C, GROUP_C)
        fr = jnp.fft.fft2(fc, axes=(1, 3), norm='ortho').real.astype(x.dtype)
        y_c = fr.reshape(b, s, WIDTH_C) @ w_proj_c[l]

        gates = jax.nn.sigmoid(h @ w_gate[l] + b_gate[l]).reshape(b, s, N_BRANCH, d)
        merged = gates[:, :, 0] * y_a + gates[:, :, 1] * y_b + gates[:, :, 2] * y_c
        x = x + merged @ w_out[l]

        h2 = rmsnorm(x, norm2_g[l])
        f_gate, f_up = jnp.split(h2 @ w_ffn_in[l], 2, axis=-1)
        x = x + (jax.nn.silu(f_gate) * f_up) @ w_ffn_out[l]
    return x
```

```python
import functools
import math

import numpy as np
import jax
import jax.numpy as jnp
from jax.experimental import pallas as pl
from jax.experimental.pallas import tpu as pltpu

D_MODEL = 1024
N_HEADS = 8
D_COMP = 32
D_V = 64
WIDTH = 512
ROPE_DIMS = 8
ROPE_HALF = 4
ROPE_THETA = 500000.0
CONV_WIDTH = 31
CONV_PAD = 15
GROUP_C = 128
D_FF = 2816
FF_CHUNK = 256
N_FF_CHUNKS = D_FF // FF_CHUNK
EPS = 1e-6

LANES = 128
VMEM_LIMIT_BYTES = 56 * 1024 * 1024

DFT_N1 = 128
DFT_N2 = 64

ROW_BLOCK = 512
HALO_ROWS = 16
Q_BLOCK = 256
K_CHUNK = 512
SAFE_SCORE_BOUND = 40.0

BF16 = jnp.bfloat16
F32 = jnp.float32


def _const_spec(shape):
    nd = len(shape)
    return pl.BlockSpec(shape, lambda *_: (0,) * nd, pipeline_mode=pl.Buffered(1))


def _split_dot(a_f32, b_bf16):
    hi = a_f32.astype(BF16)
    lo = (a_f32 - hi.astype(F32)).astype(BF16)
    return (jnp.dot(hi, b_bf16, preferred_element_type=F32)
            + jnp.dot(lo, b_bf16, preferred_element_type=F32))


def _norm_rope(t, gain, group_mean, rc, rs_next, rs_prev):
    ms = _split_dot(t * t, group_mean)
    tn = t * jax.lax.rsqrt(ms + EPS) * gain
    outs = []
    for j in range(WIDTH // LANES):
        c = tn[:, j * LANES:(j + 1) * LANES]
        outs.append(c * rc + pltpu.roll(c, LANES - ROPE_HALF, 1) * rs_next
                    + pltpu.roll(c, ROPE_HALF, 1) * rs_prev)
    return jnp.concatenate(outs, axis=1)


def _proj_kernel(x_ref, n1g_ref, win_ref, wgate_ref, bgate_ref, qg_ref, kg_ref,
                 rc_ref, rsn_ref, rsp_ref, gmean_ref, fc_ref,
                 q_ref, kt_ref, v_ref, glu_ref, zr_ref, zi_ref, gates_ref):
    x = x_ref[...]
    h = x * jax.lax.rsqrt(jnp.mean(x * x, axis=-1, keepdims=True) + EPS) * n1g_ref[...]
    hb = h.astype(BF16)

    def seg(lo, width):
        return jnp.dot(hb, win_ref[:, lo:lo + width], preferred_element_type=F32)

    rc, rsn, rsp = rc_ref[...], rsn_ref[...], rsp_ref[...]
    gmean = gmean_ref[...]
    q = _norm_rope(seg(0, WIDTH), qg_ref[...], gmean, rc, rsn, rsp)
    q_ref[...] = (q * (D_COMP ** -0.5)).astype(BF16)
    k = _norm_rope(seg(WIDTH, WIDTH), kg_ref[...], gmean, rc, rsn, rsp)
    kt_ref[0] = k.T.astype(BF16)
    v_ref[...] = seg(2 * WIDTH, WIDTH).astype(BF16)

    ga = seg(3 * WIDTH, WIDTH)
    gb = seg(4 * WIDTH, WIDTH)
    glu_ref[...] = ga * jax.nn.sigmoid(gb)

    f = seg(5 * WIDTH, WIDTH)
    fc = fc_ref[...]
    for j in range(WIDTH // GROUP_C):
        ab = jnp.dot(f[:, j * GROUP_C:(j + 1) * GROUP_C].astype(BF16), fc,
                     preferred_element_type=F32)
        zr_ref[:, j * GROUP_C:(j + 1) * GROUP_C] = ab[:, :GROUP_C]
        zi_ref[:, j * GROUP_C:(j + 1) * GROUP_C] = -ab[:, GROUP_C:]

    for j in range(3):
        lo = j * D_MODEL
        gt = jnp.dot(hb, wgate_ref[:, lo:lo + D_MODEL], preferred_element_type=F32)
        gates_ref[:, lo:lo + D_MODEL] = jax.nn.sigmoid(gt + bgate_ref[:, lo:lo + D_MODEL]).astype(BF16)


def _proj_call(x2, n1g, win, wgate, bgate, qg, kg, rc, rsn, rsp, gmean, fc, batch, seq):
    rows = x2.shape[0]
    tm = ROW_BLOCK
    blocks_per_seq = seq // tm
    row_spec = lambda w: pl.BlockSpec((tm, w), lambda i: (i, 0))
    tab_spec = pl.BlockSpec((tm, LANES), lambda i: (i % blocks_per_seq, 0))
    in_width = win.shape[1]
    return pl.pallas_call(
        _proj_kernel,
        grid=(rows // tm,),
        in_specs=[
            row_spec(D_MODEL),
            _const_spec((1, D_MODEL)),
            _const_spec((D_MODEL, in_width)),
            _const_spec((D_MODEL, 3 * D_MODEL)),
            _const_spec((1, 3 * D_MODEL)),
            _const_spec((1, WIDTH)),
            _const_spec((1, WIDTH)),
            tab_spec, tab_spec, tab_spec,
            _const_spec((WIDTH, WIDTH)),
            _const_spec((GROUP_C, 2 * GROUP_C)),
        ],
        out_specs=[
            row_spec(WIDTH),
            pl.BlockSpec((1, WIDTH, tm), lambda i: (i // blocks_per_seq, 0, i % blocks_per_seq)),
            row_spec(WIDTH),
            row_spec(WIDTH),
            row_spec(WIDTH),
            row_spec(WIDTH),
            row_spec(3 * D_MODEL),
        ],
        out_shape=[
            jax.ShapeDtypeStruct((rows, WIDTH), BF16),
            jax.ShapeDtypeStruct((batch, WIDTH, seq), BF16),
            jax.ShapeDtypeStruct((rows, WIDTH), BF16),
            jax.ShapeDtypeStruct((rows, WIDTH), F32),
            jax.ShapeDtypeStruct((rows, WIDTH), F32),
            jax.ShapeDtypeStruct((rows, WIDTH), F32),
            jax.ShapeDtypeStruct((rows, 3 * D_MODEL), BF16),
        ],
        compiler_params=pltpu.CompilerParams(
            dimension_semantics=("parallel",), vmem_limit_bytes=VMEM_LIMIT_BYTES),
        name="proj",
    )(x2, n1g, win, wgate, bgate, qg, kg, rc, rsn, rsp, gmean, fc)


def _attn_kernel(q_ref, kt_ref, v_ref, lq1_ref, lk1_ref, lq2_ref, lk2_ref, sg_ref,
                 o_ref, vext_ref, ksq_ref, m_ref, *, lam_init, seq):
    n_chunks = seq // K_CHUNK
    lane = jax.lax.broadcasted_iota(jnp.int32, (1, LANES), 1)
    first_head = lane < D_V

    @pl.when(pl.program_id(2) == 0)
    def _prepare_keys_values():
        v = v_ref[0]
        one = jnp.ones_like(v)
        vext_ref[0] = jnp.where(first_head, v, one)
        vext_ref[1] = jnp.where(first_head, one, v)
        kt = kt_ref[0].astype(F32)
        for c in range(4):
            kc = kt[c * D_COMP:(c + 1) * D_COMP, :]
            ksq = jnp.max(jnp.sum(kc * kc, axis=0, keepdims=True), axis=1, keepdims=True)
            ksq_ref[c:c + 1, :] = jnp.broadcast_to(ksq, (1, LANES))

    lam = (jnp.exp(jnp.sum(lq1_ref[...] * lk1_ref[...], axis=1, keepdims=True))
           - jnp.exp(jnp.sum(lq2_ref[...] * lk2_ref[...], axis=1, keepdims=True))
           + lam_init)

    def scores(qc, c, j):
        start = pl.multiple_of(j * K_CHUNK, K_CHUNK)
        kc = kt_ref[0, c * D_COMP:(c + 1) * D_COMP, pl.ds(start, K_CHUNK)]
        return jnp.dot(qc, kc, preferred_element_type=F32)

    def softmax_pv(c):
        head = c // 2
        qc = q_ref[0, :, c * D_COMP:(c + 1) * D_COMP]
        qf = qc.astype(F32)
        qn = jnp.sqrt(jnp.sum(qf * qf, axis=1, keepdims=True))
        bound = qn * jnp.sqrt(ksq_ref[c:c + 1, 0:1])
        m_ref[...] = bound
        unsafe = jnp.max(bound) > SAFE_SCORE_BOUND

        @pl.when(unsafe)
        def _exact_row_max():
            def mx(j, m):
                return jnp.maximum(m, jnp.max(scores(qc, c, j), axis=1, keepdims=True))
            m_ref[...] = jax.lax.fori_loop(0, n_chunks, mx, jnp.full_like(bound, -jnp.inf))

        m = m_ref[...]

        def body(j, acc):
            p = jnp.exp(scores(qc, c, j) - m).astype(BF16)
            start = pl.multiple_of(j * K_CHUNK, K_CHUNK)
            return acc + jnp.dot(p, vext_ref[head, pl.ds(start, K_CHUNK), :],
                                 preferred_element_type=F32)

        return jax.lax.fori_loop(0, n_chunks, body, jnp.zeros((Q_BLOCK, LANES), F32))

    outs = []
    for head in range(2):
        valid = first_head if head == 0 else jnp.logical_not(first_head)
        ones_col = D_V if head == 0 else 0
        a1 = softmax_pv(2 * head)
        a2 = softmax_pv(2 * head + 1)
        l1 = a1[:, ones_col:ones_col + 1]
        l2 = a2[:, ones_col:ones_col + 1]
        o = a1 / l1 - lam * (a2 / l2)
        ss = jnp.sum(jnp.where(valid, o * o, 0.0), axis=1, keepdims=True) * (1.0 / D_V)
        outs.append(o * jax.lax.rsqrt(ss + EPS))
    y = jnp.where(first_head, outs[0], outs[1]) * sg_ref[...] * (1.0 - lam_init)
    o_ref[0] = y.astype(BF16)


def _attn_call(q3, kt, v3, lq1, lk1, lq2, lk2, sg, lam_init):
    batch, seq, _ = q3.shape
    pairs = N_HEADS // 2
    vec = _const_spec((1, D_COMP))
    return pl.pallas_call(
        functools.partial(_attn_kernel, lam_init=lam_init, seq=seq),
        grid=(batch, pairs, seq // Q_BLOCK),
        in_specs=[
            pl.BlockSpec((1, Q_BLOCK, LANES), lambda b, h, i: (b, i, h)),
            pl.BlockSpec((1, LANES, seq), lambda b, h, i: (b, h, 0)),
            pl.BlockSpec((1, seq, LANES), lambda b, h, i: (b, 0, h)),
            vec, vec, vec, vec,
            _const_spec((1, LANES)),
        ],
        out_specs=pl.BlockSpec((1, Q_BLOCK, LANES), lambda b, h, i: (b, i, h)),
        out_shape=jax.ShapeDtypeStruct((batch, seq, WIDTH), BF16),
        scratch_shapes=[
            pltpu.VMEM((2, seq, LANES), BF16),
            pltpu.VMEM((8, LANES), F32),
            pltpu.VMEM((Q_BLOCK, 1), F32),
        ],
        compiler_params=pltpu.CompilerParams(
            dimension_semantics=("parallel", "parallel", "arbitrary"),
            vmem_limit_bytes=VMEM_LIMIT_BYTES),
        name="attn",
    )(q3, kt, v3, lq1, lk1, lq2, lk2, sg)


def _seqdft_kernel(zr_ref, zi_ref, m1_ref, f2_ref, o_ref, t_ref):
    def stage1(n2, carry):
        rows = pl.ds(n2, DFT_N1, stride=DFT_N2)
        z = jnp.concatenate([zr_ref[0, rows, :], zi_ref[0, rows, :]], axis=0).astype(BF16)
        t = jnp.dot(m1_ref[n2], z, preferred_element_type=F32)
        t_ref[pl.ds(pl.multiple_of(n2 * 2 * DFT_N1, 2 * DFT_N1), 2 * DFT_N1), :] = t
        return carry

    jax.lax.fori_loop(0, DFT_N2, stage1, 0)
    f2 = f2_ref[...]

    def stage2(k1, carry):
        tr = t_ref[pl.ds(k1, DFT_N2, stride=2 * DFT_N1), :]
        ti = t_ref[pl.ds(k1 + DFT_N1, DFT_N2, stride=2 * DFT_N1), :]
        t = jnp.concatenate([tr, ti], axis=0).astype(BF16)
        y = jnp.dot(f2, t, preferred_element_type=F32)
        o_ref[0, pl.ds(k1, DFT_N2, stride=DFT_N1), :] = y
        return carry

    jax.lax.fori_loop(0, DFT_N1, stage2, 0)


def _seqdft_call(zr3, zi3, m1, f2):
    batch, seq, width = zr3.shape
    z_spec = pl.BlockSpec((1, seq, LANES), lambda b, c: (b, 0, c))
    return pl.pallas_call(
        _seqdft_kernel,
        grid=(batch, width // LANES),
        in_specs=[z_spec, z_spec,
                  _const_spec((DFT_N2, 2 * DFT_N1, 2 * DFT_N1)),
                  _const_spec((DFT_N2, 2 * DFT_N2))],
        out_specs=pl.BlockSpec((1, seq, LANES), lambda b, c: (b, 0, c)),
        out_shape=jax.ShapeDtypeStruct((batch, seq, width), F32),
        scratch_shapes=[pltpu.VMEM((DFT_N2 * 2 * DFT_N1, LANES), F32)],
        compiler_params=pltpu.CompilerParams(
            dimension_semantics=("parallel", "parallel"), vmem_limit_bytes=VMEM_LIMIT_BYTES),
        name="seqdft",
    )(zr3, zi3, m1, f2)


def _dft_tables(seq, group):
    assert seq == DFT_N1 * DFT_N2
    c = np.arange(group)
    ang = 2.0 * np.pi * ((c[:, None] * c[None, :]) % group) / group
    fc = np.concatenate([np.cos(ang), np.sin(ang)], axis=1)

    k1 = np.arange(DFT_N1)[None, :, None]
    n1 = np.arange(DFT_N1)[None, None, :]
    n2 = np.arange(DFT_N2)[:, None, None]
    th = 2.0 * np.pi * ((k1 * (DFT_N2 * n1 + n2)) % seq) / seq
    co, si = np.cos(th), np.sin(th)
    m1 = np.concatenate([np.concatenate([co, si], axis=2),
                         np.concatenate([-si, co], axis=2)], axis=1)

    k2 = np.arange(DFT_N2)
    ph = 2.0 * np.pi * ((k2[:, None] * k2[None, :]) % DFT_N2) / DFT_N2
    f2 = np.concatenate([np.cos(ph), np.sin(ph)], axis=1) / math.sqrt(seq * group)
    return tuple(jnp.asarray(t, dtype=F32).astype(BF16) for t in (fc, m1, f2))


def _merge_kernel(x_ref, att_ref, glu_ref, gprev_ref, gnext_ref, fr_ref, gates_ref,
                  cw_ref, cb_ref, lng_ref, lnb_ref, wpa_ref, wpb_ref, wpc_ref, wout_ref,
                  n2g_ref, wfi_ref, wfo_ref, o_ref, win_ref, cv_ref, *, blocks_per_seq):
    tm = ROW_BLOCK
    i = pl.program_id(0)
    pos = i % blocks_per_seq
    win_ref[0:HALO_ROWS, :] = jnp.where(pos == 0, 0.0, gprev_ref[...])
    win_ref[HALO_ROWS:HALO_ROWS + tm, :] = glu_ref[...]
    win_ref[HALO_ROWS + tm:, :] = jnp.where(pos == blocks_per_seq - 1, 0.0, gnext_ref[...])

    rc = 32
    for r0 in range(0, tm, rc):
        acc = jnp.zeros((rc, WIDTH), F32)
        for t in range(CONV_WIDTH):
            off = r0 + t + HALO_ROWS - CONV_PAD
            acc = acc + cw_ref[t:t + 1, :] * win_ref[off:off + rc, :]
        cv_ref[r0:r0 + rc, :] = acc

    cv = cv_ref[...] + cb_ref[...]
    mu = jnp.mean(cv, axis=-1, keepdims=True)
    d = cv - mu
    var = jnp.mean(d * d, axis=-1, keepdims=True)
    cv = d * jax.lax.rsqrt(var + EPS) * lng_ref[...] + lnb_ref[...]
    sb = (cv * jax.nn.sigmoid(cv)).astype(BF16)

    y_a = jnp.dot(att_ref[...], wpa_ref[...], preferred_element_type=F32)
    y_b = jnp.dot(sb, wpb_ref[...], preferred_element_type=F32)
    y_c = jnp.dot(fr_ref[...].astype(BF16), wpc_ref[...], preferred_element_type=F32)
    merged = (gates_ref[:, 0:D_MODEL].astype(F32) * y_a
              + gates_ref[:, D_MODEL:2 * D_MODEL].astype(F32) * y_b
              + gates_ref[:, 2 * D_MODEL:].astype(F32) * y_c)
    x1 = x_ref[...] + jnp.dot(merged.astype(BF16), wout_ref[...], preferred_element_type=F32)

    h2 = x1 * jax.lax.rsqrt(jnp.mean(x1 * x1, axis=-1, keepdims=True) + EPS) * n2g_ref[...]
    h2b = h2.astype(BF16)
    acc = jnp.zeros((tm, D_MODEL), F32)
    for j in range(N_FF_CHUNKS):
        gu = jnp.dot(h2b, wfi_ref[j], preferred_element_type=F32)
        gt = gu[:, :FF_CHUNK]
        a = (gt * jax.nn.sigmoid(gt) * gu[:, FF_CHUNK:]).astype(BF16)
        acc = acc + jnp.dot(a, wfo_ref[j], preferred_element_type=F32)
    o_ref[...] = x1 + acc


def _merge_call(x2, att, glu, fr, gates, cw, cb, lng, lnb, wpa, wpb, wpc, wout, n2g, wfi, wfo, seq):
    rows = x2.shape[0]
    tm = ROW_BLOCK
    blocks_per_seq = seq // tm
    halo_per_block = tm // HALO_ROWS
    n_halo_blocks = rows // HALO_ROWS
    row_spec = lambda w: pl.BlockSpec((tm, w), lambda i: (i, 0))
    prev_spec = pl.BlockSpec((HALO_ROWS, WIDTH),
                             lambda i: (jnp.maximum(i * halo_per_block - 1, 0), 0))
    next_spec = pl.BlockSpec((HALO_ROWS, WIDTH),
                             lambda i: (jnp.minimum((i + 1) * halo_per_block, n_halo_blocks - 1), 0))
    return pl.pallas_call(
        functools.partial(_merge_kernel, blocks_per_seq=blocks_per_seq),
        grid=(rows // tm,),
        in_specs=[
            row_spec(D_MODEL), row_spec(WIDTH), row_spec(WIDTH), prev_spec, next_spec,
            row_spec(WIDTH), row_spec(3 * D_MODEL),
            _const_spec((CONV_WIDTH + 1, WIDTH)),
            _const_spec((1, WIDTH)), _const_spec((1, WIDTH)), _const_spec((1, WIDTH)),
            _const_spec((WIDTH, D_MODEL)), _const_spec((WIDTH, D_MODEL)), _const_spec((WIDTH, D_MODEL)),
            _const_spec((D_MODEL, D_MODEL)),
            _const_spec((1, D_MODEL)),
            _const_spec((N_FF_CHUNKS, D_MODEL, 2 * FF_CHUNK)),
            _const_spec((N_FF_CHUNKS, FF_CHUNK, D_MODEL)),
        ],
        out_specs=row_spec(D_MODEL),
        out_shape=jax.ShapeDtypeStruct((rows, D_MODEL), F32),
        scratch_shapes=[pltpu.VMEM((tm + 2 * HALO_ROWS, WIDTH), F32),
                        pltpu.VMEM((tm, WIDTH), F32)],
        compiler_params=pltpu.CompilerParams(
            dimension_semantics=("parallel",), vmem_limit_bytes=VMEM_LIMIT_BYTES),
        name="merge_ffn",
    )(x2, att, glu, glu, glu, fr, gates, cw, cb, lng, lnb, wpa, wpb, wpc, wout, n2g, wfi, wfo)


def _rope_lane_tables(seq):
    pos = jnp.arange(seq, dtype=F32)
    inv = 1.0 / (ROPE_THETA ** (jnp.arange(0, ROPE_DIMS, 2, dtype=F32) / ROPE_DIMS))
    ang = pos[:, None] * inv[None, :]
    cos, sin = jnp.cos(ang), jnp.sin(ang)
    d = np.arange(LANES) % D_COMP
    f = d % ROPE_HALF
    rc = jnp.where(d[None, :] < ROPE_DIMS, cos[:, f], 1.0)
    rs_next = jnp.where(d[None, :] < ROPE_HALF, -sin[:, f], 0.0)
    rs_prev = jnp.where((d[None, :] >= ROPE_HALF) & (d[None, :] < ROPE_DIMS), sin[:, f], 0.0)
    return rc.astype(F32), rs_next.astype(F32), rs_prev.astype(F32)


def kernel(x, norm1_g, w_in, qnorm_g, knorm_g, lambda_q1, lambda_k1, lambda_q2, lambda_k2, subln_g,
           w_proj_a, conv_w, conv_b, conv_ln_g, conv_ln_b, w_proj_b, w_proj_c, w_gate, b_gate, w_out,
           norm2_g, w_ffn_in, w_ffn_out):
    batch, seq, d = x.shape
    depth = w_in.shape[0]
    rows = batch * seq
    rc, rsn, rsp = _rope_lane_tables(seq)
    fc, m1, f2 = _dft_tables(seq, GROUP_C)
    gidx = np.arange(WIDTH) // D_COMP
    gmean = jnp.asarray((gidx[:, None] == gidx[None, :]) / D_COMP, dtype=F32).astype(BF16)

    x2 = x.reshape(rows, d)
    for l in range(depth):
        lam_init = 0.8 - 0.6 * math.exp(-0.3 * l)
        row = lambda a: a[l].reshape(1, -1)
        q, kt, v, glu, zr, zi, gates = _proj_call(
            x2, row(norm1_g), w_in[l].astype(BF16), w_gate[l].astype(BF16), row(b_gate),
            jnp.tile(qnorm_g[l], WIDTH // D_COMP).reshape(1, WIDTH),
            jnp.tile(knorm_g[l], WIDTH // D_COMP).reshape(1, WIDTH),
            rc, rsn, rsp, gmean, fc, batch, seq)
        att = _attn_call(
            q.reshape(batch, seq, WIDTH), kt, v.reshape(batch, seq, WIDTH),
            row(lambda_q1), row(lambda_k1), row(lambda_q2), row(lambda_k2),
            jnp.tile(subln_g[l], LANES // D_V).reshape(1, LANES), lam_init)
        fr = _seqdft_call(zr.reshape(batch, seq, WIDTH), zi.reshape(batch, seq, WIDTH), m1, f2)
        wfi = w_ffn_in[l].astype(BF16)
        wfi = jnp.concatenate([wfi[:, :D_FF].reshape(d, N_FF_CHUNKS, FF_CHUNK),
                               wfi[:, D_FF:].reshape(d, N_FF_CHUNKS, FF_CHUNK)], axis=2)
        wfi = wfi.transpose(1, 0, 2)
        wfo = w_ffn_out[l].astype(BF16).reshape(N_FF_CHUNKS, FF_CHUNK, d)
        cw = jnp.concatenate([conv_w[l], jnp.zeros((1, WIDTH), F32)], axis=0)
        x2 = _merge_call(
            x2, att.reshape(rows, WIDTH), glu, fr.reshape(rows, WIDTH), gates,
            cw, row(conv_b), row(conv_ln_g), row(conv_ln_b),
            w_proj_a[l].astype(BF16), w_proj_b[l].astype(BF16), w_proj_c[l].astype(BF16),
            w_out[l].astype(BF16), row(norm2_g), wfi, wfo, seq)
    return x2.reshape(batch, seq, d)
```

```python
import functools
import math

import numpy as np
import jax
import jax.numpy as jnp
from jax.experimental import pallas as pl
from jax.experimental.pallas import tpu as pltpu

D_MODEL = 1024
N_HEADS = 8
D_COMP = 32
D_V = 64
WIDTH = 512
ROPE_DIMS = 8
ROPE_HALF = 4
ROPE_THETA = 500000.0
CONV_WIDTH = 31
CONV_PAD = 15
GROUP_C = 128
D_FF = 2816
FF_CHUNK = 256
N_FF_CHUNKS = D_FF // FF_CHUNK
EPS = 1e-6

LANES = 128
SUBLANES = 8
VMEM_LIMIT_BYTES = 56 * 1024 * 1024

DFT_N1 = 128
DFT_N2 = 64

ROW_BLOCK = 512
HALO_ROWS = 16
CONV_ROW_CHUNK = 64
Q_BLOCK = 256
K_SUB = 256
K_CHUNK = 2048
K_AUG = 64
SAFE_SCORE_BOUND = 40.0
SCORE_BOUND_MARGIN = 1.02
LOG2_E = math.log2(math.e)

BF16 = jnp.bfloat16
F32 = jnp.float32


def _const_spec(shape):
    nd = len(shape)
    return pl.BlockSpec(shape, lambda *_: (0,) * nd, pipeline_mode=pl.Buffered(1))


def _split_dot(a_f32, b_bf16):
    hi = a_f32.astype(BF16)
    lo = (a_f32 - hi.astype(F32)).astype(BF16)
    return (jnp.dot(hi, b_bf16, preferred_element_type=F32)
            + jnp.dot(lo, b_bf16, preferred_element_type=F32))


def _norm_rope(t, gain, group_mean, rc, rs_next, rs_prev):
    ms = _split_dot(t * t, group_mean)
    tn = t * jax.lax.rsqrt(ms + EPS) * gain
    outs = []
    for j in range(WIDTH // LANES):
        c = tn[:, j * LANES:(j + 1) * LANES]
        outs.append(c * rc + pltpu.roll(c, LANES - ROPE_HALF, 1) * rs_next
                    + pltpu.roll(c, ROPE_HALF, 1) * rs_prev)
    return jnp.concatenate(outs, axis=1)


def _proj_kernel(x_ref, n1g_ref, win_ref, wgate_ref, bgate_ref, qg_ref, kg_ref,
                 rc_ref, rsn_ref, rsp_ref, gmean_ref, fc_ref,
                 q_ref, kt_ref, v_ref, glu_ref, zr_ref, zi_ref, gates_ref):
    x = x_ref[...]
    h = x * jax.lax.rsqrt(jnp.mean(x * x, axis=-1, keepdims=True) + EPS) * n1g_ref[...]
    hb = h.astype(BF16)

    def seg(lo, width):
        return jnp.dot(hb, win_ref[:, lo:lo + width], preferred_element_type=F32)

    rc, rsn, rsp = rc_ref[...], rsn_ref[...], rsp_ref[...]
    gmean = gmean_ref[...]
    q = _norm_rope(seg(0, WIDTH), qg_ref[...], gmean, rc, rsn, rsp)
    q_ref[...] = (q * (D_COMP ** -0.5 * LOG2_E)).astype(BF16)
    k = _norm_rope(seg(WIDTH, WIDTH), kg_ref[...], gmean, rc, rsn, rsp)
    kt_ref[0] = k.T.astype(BF16)
    v_ref[...] = seg(2 * WIDTH, WIDTH).astype(BF16)

    ga = seg(3 * WIDTH, WIDTH)
    gb = seg(4 * WIDTH, WIDTH)
    glu_ref[...] = ga * jax.nn.sigmoid(gb)

    f = seg(5 * WIDTH, WIDTH)
    fc = fc_ref[...]
    for j in range(WIDTH // GROUP_C):
        ab = jnp.dot(f[:, j * GROUP_C:(j + 1) * GROUP_C].astype(BF16), fc,
                     preferred_element_type=F32)
        zr_ref[:, j * GROUP_C:(j + 1) * GROUP_C] = ab[:, :GROUP_C]
        zi_ref[:, j * GROUP_C:(j + 1) * GROUP_C] = -ab[:, GROUP_C:]

    for j in range(3):
        lo = j * D_MODEL
        gt = jnp.dot(hb, wgate_ref[:, lo:lo + D_MODEL], preferred_element_type=F32)
        gates_ref[:, lo:lo + D_MODEL] = jax.nn.sigmoid(gt + bgate_ref[:, lo:lo + D_MODEL]).astype(BF16)


def _proj_call(x2, n1g, win, wgate, bgate, qg, kg, rc, rsn, rsp, gmean, fc, batch, seq):
    rows = x2.shape[0]
    tm = ROW_BLOCK
    blocks_per_seq = seq // tm
    row_spec = lambda w: pl.BlockSpec((tm, w), lambda i: (i, 0))
    tab_spec = pl.BlockSpec((tm, LANES), lambda i: (i % blocks_per_seq, 0))
    in_width = win.shape[1]
    return pl.pallas_call(
        _proj_kernel,
        grid=(rows // tm,),
        in_specs=[
            row_spec(D_MODEL),
            _const_spec((1, D_MODEL)),
            _const_spec((D_MODEL, in_width)),
            _const_spec((D_MODEL, 3 * D_MODEL)),
            _const_spec((1, 3 * D_MODEL)),
            _const_spec((1, WIDTH)),
            _const_spec((1, WIDTH)),
            tab_spec, tab_spec, tab_spec,
            _const_spec((WIDTH, WIDTH)),
            _const_spec((GROUP_C, 2 * GROUP_C)),
        ],
        out_specs=[
            row_spec(WIDTH),
            pl.BlockSpec((1, WIDTH, tm), lambda i: (i // blocks_per_seq, 0, i % blocks_per_seq)),
            row_spec(WIDTH),
            row_spec(WIDTH),
            row_spec(WIDTH),
            row_spec(WIDTH),
            row_spec(3 * D_MODEL),
        ],
        out_shape=[
            jax.ShapeDtypeStruct((rows, WIDTH), BF16),
            jax.ShapeDtypeStruct((batch, WIDTH, seq), BF16),
            jax.ShapeDtypeStruct((rows, WIDTH), BF16),
            jax.ShapeDtypeStruct((rows, WIDTH), F32),
            jax.ShapeDtypeStruct((rows, WIDTH), F32),
            jax.ShapeDtypeStruct((rows, WIDTH), F32),
            jax.ShapeDtypeStruct((rows, 3 * D_MODEL), BF16),
        ],
        compiler_params=pltpu.CompilerParams(
            dimension_semantics=("parallel",), vmem_limit_bytes=VMEM_LIMIT_BYTES),
        name="proj",
    )(x2, n1g, win, wgate, bgate, qg, kg, rc, rsn, rsp, gmean, fc)


def _attn_kernel(bound_ref, q_ref, kt_ref, v_ref, lq1_ref, lk1_ref, lq2_ref, lk2_ref, sg_ref,
                 o_ref, vext_ref, kaug_ref, col_ref, acc_ref, *, lam_init, seq):
    tq = Q_BLOCK
    lane = jax.lax.broadcasted_iota(jnp.int32, (1, LANES), 1)
    first_head = lane < D_V
    comp_lanes = lane < D_COMP
    bound = bound_ref[0]
    use_row_max = bound > SAFE_SCORE_BOUND * LOG2_E

    @pl.when(pl.program_id(2) == 0)
    def _prepare_keys_values():
        v = v_ref[0]
        one = jnp.ones_like(v)
        vext_ref[0] = jnp.where(first_head, v, one)
        vext_ref[1] = jnp.where(first_head, one, v)
        row = jax.lax.broadcasted_iota(jnp.int32, (K_AUG - D_COMP, seq), 0)
        extra = jnp.where(row == 0, jnp.where(use_row_max, 1.0, -bound), 0.0).astype(BF16)
        for c in range(4):
            kaug_ref[c, 0:D_COMP, :] = kt_ref[0, c * D_COMP:(c + 1) * D_COMP, :]
            kaug_ref[c, D_COMP:, :] = extra

    lam = (jnp.exp(jnp.sum(lq1_ref[...] * lk1_ref[...], axis=1, keepdims=True))
           - jnp.exp(jnp.sum(lq2_ref[...] * lk2_ref[...], axis=1, keepdims=True))
           + lam_init)
    qf = q_ref[0].astype(F32)

    def shifted_query(c):
        qc = qf if c == 0 else pltpu.roll(qf, LANES - c * D_COMP, 1)
        qc = jnp.where(comp_lanes, qc, 0.0)
        col_ref[...] = jnp.ones_like(col_ref)

        @pl.when(use_row_max)
        def _exact_row_max():
            q0 = qc[:, :K_AUG].astype(BF16)

            def mx(j, m):
                start = pl.multiple_of(j * K_SUB, K_SUB)
                s = jnp.dot(q0, kaug_ref[c, :, pl.ds(start, K_SUB)], preferred_element_type=F32)
                return jnp.maximum(m, jnp.max(s, axis=1, keepdims=True))
            m = jax.lax.fori_loop(0, seq // K_SUB, mx, jnp.full((tq, 1), -jnp.inf, F32))
            col_ref[...] = -m

        qa = jnp.where(lane == D_COMP, col_ref[...], qc)
        return qa[:, :K_AUG].astype(BF16)

    def softmax_pv(head):
        qa1 = shifted_query(2 * head)
        qa2 = shifted_query(2 * head + 1)
        acc_ref[...] = jnp.zeros_like(acc_ref)

        def body(j, carry):
            pv = None
            for u in range(K_CHUNK // K_SUB):
                start = pl.multiple_of(j * K_CHUNK + u * K_SUB, K_SUB)
                s1 = jnp.dot(qa1, kaug_ref[2 * head, :, pl.ds(start, K_SUB)],
                             preferred_element_type=F32)
                s2 = jnp.dot(qa2, kaug_ref[2 * head + 1, :, pl.ds(start, K_SUB)],
                             preferred_element_type=F32)
                p = jnp.concatenate([jnp.exp2(s1), jnp.exp2(s2)], axis=0).astype(BF16)
                d = jnp.dot(p, vext_ref[head, pl.ds(start, K_SUB), :], preferred_element_type=F32)
                pv = d if pv is None else pv + d
            acc_ref[...] += pv
            return carry

        jax.lax.fori_loop(0, seq // K_CHUNK, body, 0)
        return acc_ref[...]

    outs = []
    for head in range(2):
        valid = first_head if head == 0 else jnp.logical_not(first_head)
        acc = softmax_pv(head)
        a = acc / pltpu.roll(acc, D_V, 1)
        o = a[:tq] - lam * a[tq:]
        ss = jnp.sum(jnp.where(valid, o * o, 0.0), axis=1, keepdims=True) * (1.0 / D_V)
        outs.append(o * jax.lax.rsqrt(ss + EPS))
    y = jnp.where(first_head, outs[0], outs[1]) * sg_ref[...] * (1.0 - lam_init)
    o_ref[0] = y.astype(BF16)


def _attn_call(score_bound, q3, kt, v3, lq1, lk1, lq2, lk2, sg, lam_init):
    batch, seq, _ = q3.shape
    pairs = N_HEADS // 2
    vec = _const_spec((1, D_COMP))
    return pl.pallas_call(
        functools.partial(_attn_kernel, lam_init=lam_init, seq=seq),
        grid=(batch, pairs, seq // Q_BLOCK),
        in_specs=[
            pl.BlockSpec(memory_space=pltpu.SMEM),
            pl.BlockSpec((1, Q_BLOCK, LANES), lambda b, h, i: (b, i, h)),
            pl.BlockSpec((1, LANES, seq), lambda b, h, i: (b, h, 0)),
            pl.BlockSpec((1, seq, LANES), lambda b, h, i: (b, 0, h)),
            vec, vec, vec, vec,
            _const_spec((1, LANES)),
        ],
        out_specs=pl.BlockSpec((1, Q_BLOCK, LANES), lambda b, h, i: (b, i, h)),
        out_shape=jax.ShapeDtypeStruct((batch, seq, WIDTH), BF16),
        scratch_shapes=[
            pltpu.VMEM((2, seq, LANES), BF16),
            pltpu.VMEM((4, K_AUG, seq), BF16),
            pltpu.VMEM((Q_BLOCK, 1), F32),
            pltpu.VMEM((2 * Q_BLOCK, LANES), F32),
        ],
        compiler_params=pltpu.CompilerParams(
            dimension_semantics=("parallel", "parallel", "arbitrary"),
            vmem_limit_bytes=VMEM_LIMIT_BYTES),
        name="attn",
    )(score_bound, q3, kt, v3, lq1, lk1, lq2, lk2, sg)


def _seqdft_kernel(zr_ref, zi_ref, m1_ref, f2_ref, o_ref, t_ref):
    def stage1(n2, carry):
        rows = pl.ds(n2, DFT_N1, stride=DFT_N2)
        z = jnp.concatenate([zr_ref[0, rows, :], zi_ref[0, rows, :]], axis=0).astype(BF16)
        t = jnp.dot(m1_ref[n2], z, preferred_element_type=F32)
        t_ref[pl.ds(pl.multiple_of(n2 * 2 * DFT_N1, 2 * DFT_N1), 2 * DFT_N1), :] = t
        return carry

    jax.lax.fori_loop(0, DFT_N2, stage1, 0, unroll=4)
    f2 = f2_ref[...]

    def stage2(k1, carry):
        tr = t_ref[pl.ds(k1, DFT_N2, stride=2 * DFT_N1), :]
        ti = t_ref[pl.ds(k1 + DFT_N1, DFT_N2, stride=2 * DFT_N1), :]
        t = jnp.concatenate([tr, ti], axis=0).astype(BF16)
        y = jnp.dot(f2, t, preferred_element_type=F32)
        o_ref[0, pl.ds(k1, DFT_N2, stride=DFT_N1), :] = y
        return carry

    jax.lax.fori_loop(0, DFT_N1, stage2, 0, unroll=8)


def _seqdft_call(zr3, zi3, m1, f2):
    batch, seq, width = zr3.shape
    z_spec = pl.BlockSpec((1, seq, LANES), lambda b, c: (b, 0, c))
    return pl.pallas_call(
        _seqdft_kernel,
        grid=(batch, width // LANES),
        in_specs=[z_spec, z_spec,
                  _const_spec((DFT_N2, 2 * DFT_N1, 2 * DFT_N1)),
                  _const_spec((DFT_N2, 2 * DFT_N2))],
        out_specs=pl.BlockSpec((1, seq, LANES), lambda b, c: (b, 0, c)),
        out_shape=jax.ShapeDtypeStruct((batch, seq, width), F32),
        scratch_shapes=[pltpu.VMEM((DFT_N2 * 2 * DFT_N1, LANES), F32)],
        compiler_params=pltpu.CompilerParams(
            dimension_semantics=("parallel", "parallel"), vmem_limit_bytes=VMEM_LIMIT_BYTES),
        name="seqdft",
    )(zr3, zi3, m1, f2)


def _dft_tables(seq, group):
    assert seq == DFT_N1 * DFT_N2
    c = np.arange(group)
    ang = 2.0 * np.pi * ((c[:, None] * c[None, :]) % group) / group
    fc = np.concatenate([np.cos(ang), np.sin(ang)], axis=1)

    k1 = np.arange(DFT_N1)[None, :, None]
    n1 = np.arange(DFT_N1)[None, None, :]
    n2 = np.arange(DFT_N2)[:, None, None]
    th = 2.0 * np.pi * ((k1 * (DFT_N2 * n1 + n2)) % seq) / seq
    co, si = np.cos(th), np.sin(th)
    m1 = np.concatenate([np.concatenate([co, si], axis=2),
                         np.concatenate([-si, co], axis=2)], axis=1)

    k2 = np.arange(DFT_N2)
    ph = 2.0 * np.pi * ((k2[:, None] * k2[None, :]) % DFT_N2) / DFT_N2
    f2 = np.concatenate([np.cos(ph), np.sin(ph)], axis=1) / math.sqrt(seq * group)
    return tuple(jnp.asarray(t, dtype=F32).astype(BF16) for t in (fc, m1, f2))


def _merge_kernel(x_ref, att_ref, glu_ref, gprev_ref, gnext_ref, fr_ref, gates_ref,
                  cw_ref, cb_ref, lng_ref, lnb_ref, wpa_ref, wpb_ref, wpc_ref, wout_ref,
                  n2g_ref, wfi_ref, wfo_ref, o_ref, win_ref, cv_ref, *, blocks_per_seq):
    tm = ROW_BLOCK
    i = pl.program_id(0)
    pos = i % blocks_per_seq
    win_ref[0:HALO_ROWS, :] = jnp.where(pos == 0, 0.0, gprev_ref[...])
    win_ref[HALO_ROWS:HALO_ROWS + tm, :] = glu_ref[...]
    win_ref[HALO_ROWS + tm:, :] = jnp.where(pos == blocks_per_seq - 1, 0.0, gnext_ref[...])

    assert HALO_ROWS - CONV_PAD == 1
    rc = CONV_ROW_CHUNK
    for c0 in range(0, WIDTH, LANES):
        for r0 in range(0, tm, rc):
            acc = None
            for b in range(SUBLANES):
                part = None
                for a in range(-(-(CONV_WIDTH + 1) // SUBLANES)):
                    t = SUBLANES * a + b - 1
                    if 0 <= t < CONV_WIDTH:
                        lo = r0 + SUBLANES * a
                        term = (cw_ref[t:t + 1, c0:c0 + LANES]
                                * win_ref[lo:lo + rc + SUBLANES, c0:c0 + LANES])
                        part = term if part is None else part + term
                shifted = part[b:b + rc]
                acc = shifted if acc is None else acc + shifted
            cv_ref[r0:r0 + rc, c0:c0 + LANES] = acc

    cv = cv_ref[...] + cb_ref[...]
    mu = jnp.mean(cv, axis=-1, keepdims=True)
    d = cv - mu
    var = jnp.mean(d * d, axis=-1, keepdims=True)
    cv = d * jax.lax.rsqrt(var + EPS) * lng_ref[...] + lnb_ref[...]
    sb = (cv * jax.nn.sigmoid(cv)).astype(BF16)

    y_a = jnp.dot(att_ref[...], wpa_ref[...], preferred_element_type=F32)
    y_b = jnp.dot(sb, wpb_ref[...], preferred_element_type=F32)
    y_c = jnp.dot(fr_ref[...].astype(BF16), wpc_ref[...], preferred_element_type=F32)
    merged = (gates_ref[:, 0:D_MODEL].astype(F32) * y_a
              + gates_ref[:, D_MODEL:2 * D_MODEL].astype(F32) * y_b
              + gates_ref[:, 2 * D_MODEL:].astype(F32) * y_c)
    x1 = x_ref[...] + jnp.dot(merged.astype(BF16), wout_ref[...], preferred_element_type=F32)

    h2 = x1 * jax.lax.rsqrt(jnp.mean(x1 * x1, axis=-1, keepdims=True) + EPS) * n2g_ref[...]
    h2b = h2.astype(BF16)
    acc = jnp.zeros((tm, D_MODEL), F32)
    for j in range(N_FF_CHUNKS):
        gu = jnp.dot(h2b, wfi_ref[j], preferred_element_type=F32)
        gt = gu[:, :FF_CHUNK]
        a = (gt * jax.nn.sigmoid(gt) * gu[:, FF_CHUNK:]).astype(BF16)
        acc = acc + jnp.dot(a, wfo_ref[j], preferred_element_type=F32)
    o_ref[...] = x1 + acc


def _merge_call(x2, att, glu, fr, gates, cw, cb, lng, lnb, wpa, wpb, wpc, wout, n2g, wfi, wfo, seq):
    rows = x2.shape[0]
    tm = ROW_BLOCK
    blocks_per_seq = seq // tm
    halo_per_block = tm // HALO_ROWS
    n_halo_blocks = rows // HALO_ROWS
    row_spec = lambda w: pl.BlockSpec((tm, w), lambda i: (i, 0))
    prev_spec = pl.BlockSpec((HALO_ROWS, WIDTH),
                             lambda i: (jnp.maximum(i * halo_per_block - 1, 0), 0))
    next_spec = pl.BlockSpec((HALO_ROWS, WIDTH),
                             lambda i: (jnp.minimum((i + 1) * halo_per_block, n_halo_blocks - 1), 0))
    return pl.pallas_call(
        functools.partial(_merge_kernel, blocks_per_seq=blocks_per_seq),
        grid=(rows // tm,),
        in_specs=[
            row_spec(D_MODEL), row_spec(WIDTH), row_spec(WIDTH), prev_spec, next_spec,
            row_spec(WIDTH), row_spec(3 * D_MODEL),
            _const_spec((CONV_WIDTH + 1, WIDTH)),
            _const_spec((1, WIDTH)), _const_spec((1, WIDTH)), _const_spec((1, WIDTH)),
            _const_spec((WIDTH, D_MODEL)), _const_spec((WIDTH, D_MODEL)), _const_spec((WIDTH, D_MODEL)),
            _const_spec((D_MODEL, D_MODEL)),
            _const_spec((1, D_MODEL)),
            _const_spec((N_FF_CHUNKS, D_MODEL, 2 * FF_CHUNK)),
            _const_spec((N_FF_CHUNKS, FF_CHUNK, D_MODEL)),
        ],
        out_specs=row_spec(D_MODEL),
        out_shape=jax.ShapeDtypeStruct((rows, D_MODEL), F32),
        scratch_shapes=[pltpu.VMEM((tm + 2 * HALO_ROWS, WIDTH), F32),
                        pltpu.VMEM((tm, WIDTH), F32)],
        compiler_params=pltpu.CompilerParams(
            dimension_semantics=("parallel",), vmem_limit_bytes=VMEM_LIMIT_BYTES),
        name="merge_ffn",
    )(x2, att, glu, glu, glu, fr, gates, cw, cb, lng, lnb, wpa, wpb, wpc, wout, n2g, wfi, wfo)


def _rope_lane_tables(seq):
    pos = jnp.arange(seq, dtype=F32)
    inv = 1.0 / (ROPE_THETA ** (jnp.arange(0, ROPE_DIMS, 2, dtype=F32) / ROPE_DIMS))
    ang = pos[:, None] * inv[None, :]
    cos, sin = jnp.cos(ang), jnp.sin(ang)
    d = np.arange(LANES) % D_COMP
    f = d % ROPE_HALF
    rc = jnp.where(d[None, :] < ROPE_DIMS, cos[:, f], 1.0)
    rs_next = jnp.where(d[None, :] < ROPE_HALF, -sin[:, f], 0.0)
    rs_prev = jnp.where((d[None, :] >= ROPE_HALF) & (d[None, :] < ROPE_DIMS), sin[:, f], 0.0)
    return rc.astype(F32), rs_next.astype(F32), rs_prev.astype(F32)


def kernel(x, norm1_g, w_in, qnorm_g, knorm_g, lambda_q1, lambda_k1, lambda_q2, lambda_k2, subln_g,
           w_proj_a, conv_w, conv_b, conv_ln_g, conv_ln_b, w_proj_b, w_proj_c, w_gate, b_gate, w_out,
           norm2_g, w_ffn_in, w_ffn_out):
    batch, seq, d = x.shape
    depth = w_in.shape[0]
    rows = batch * seq
    rc, rsn, rsp = _rope_lane_tables(seq)
    fc, m1, f2 = _dft_tables(seq, GROUP_C)
    gidx = np.arange(WIDTH) // D_COMP
    gmean = jnp.asarray((gidx[:, None] == gidx[None, :]) / D_COMP, dtype=F32).astype(BF16)

    x2 = x.reshape(rows, d)
    for l in range(depth):
        lam_init = 0.8 - 0.6 * math.exp(-0.3 * l)
        row = lambda a: a[l].reshape(1, -1)
        q, kt, v, glu, zr, zi, gates = _proj_call(
            x2, row(norm1_g), w_in[l].astype(BF16), w_gate[l].astype(BF16), row(b_gate),
            jnp.tile(qnorm_g[l], WIDTH // D_COMP).reshape(1, WIDTH),
            jnp.tile(knorm_g[l], WIDTH // D_COMP).reshape(1, WIDTH),
            rc, rsn, rsp, gmean, fc, batch, seq)
        score_bound = (SCORE_BOUND_MARGIN * LOG2_E * math.sqrt(D_COMP) * jnp.max(jnp.abs(qnorm_g[l]))
                       * jnp.max(jnp.abs(knorm_g[l]))).reshape(1).astype(F32)
        att = _attn_call(
            score_bound, q.reshape(batch, seq, WIDTH), kt, v.reshape(batch, seq, WIDTH),
            row(lambda_q1), row(lambda_k1), row(lambda_q2), row(lambda_k2),
            jnp.tile(subln_g[l], LANES // D_V).reshape(1, LANES), lam_init)
        fr = _seqdft_call(zr.reshape(batch, seq, WIDTH), zi.reshape(batch, seq, WIDTH), m1, f2)
        wfi = w_ffn_in[l].astype(BF16)
        wfi = jnp.concatenate([wfi[:, :D_FF].reshape(d, N_FF_CHUNKS, FF_CHUNK),
                               wfi[:, D_FF:].reshape(d, N_FF_CHUNKS, FF_CHUNK)], axis=2)
        wfi = wfi.transpose(1, 0, 2)
        wfo = w_ffn_out[l].astype(BF16).reshape(N_FF_CHUNKS, FF_CHUNK, d)
        cw = jnp.concatenate([conv_w[l], jnp.zeros((1, WIDTH), F32)], axis=0)
        x2 = _merge_call(
            x2, att.reshape(rows, WIDTH), glu, fr.reshape(rows, WIDTH), gates,
            cw, row(conv_b), row(conv_ln_g), row(conv_ln_b),
            w_proj_a[l].astype(BF16), w_proj_b[l].astype(BF16), w_proj_c[l].astype(BF16),
            w_out[l].astype(BF16), row(norm2_g), wfi, wfo, seq)
    return x2.reshape(batch, seq, d)
```

```python
import functools
import math

import numpy as np
import jax
import jax.numpy as jnp
from jax.experimental import pallas as pl
from jax.experimental.pallas import tpu as pltpu

D_MODEL = 1024
N_HEADS = 8
D_COMP = 32
D_V = 64
WIDTH = 512
ROPE_DIMS = 8
ROPE_HALF = 4
ROPE_THETA = 500000.0
CONV_WIDTH = 31
CONV_PAD = 15
GROUP_C = 128
D_FF = 2816
FF_CHUNK = 256
N_FF_CHUNKS = D_FF // FF_CHUNK
EPS = 1e-6

LANES = 128
SUBLANES = 8
MXU_DIM = 256
VMEM_LIMIT_BYTES = 56 * 1024 * 1024

DFT_N1 = 128
DFT_N2 = 64

ROW_BLOCK = 512
HALO_ROWS = 16
CONV_ROW_CHUNK = 64
Q_BLOCK = 1024
K_SUB = 256
K_CHUNK = 2048
K_AUG = 64
SAFE_SCORE_BOUND = 40.0
SCORE_BOUND_MARGIN = 1.02
LOG2_E = math.log2(math.e)

BF16 = jnp.bfloat16
F32 = jnp.float32


def _const_spec(shape):
    nd = len(shape)
    return pl.BlockSpec(shape, lambda *_: (0,) * nd, pipeline_mode=pl.Buffered(1))


def _split_dot(a_f32, b_bf16):
    hi = a_f32.astype(BF16)
    lo = (a_f32 - hi.astype(F32)).astype(BF16)
    return (jnp.dot(hi, b_bf16, preferred_element_type=F32)
            + jnp.dot(lo, b_bf16, preferred_element_type=F32))


def _norm_rope(t, gain, group_mean, rc, rs_next, rs_prev):
    sq = t * t
    ms = jnp.concatenate([_split_dot(sq[:, j:j + MXU_DIM], group_mean)
                          for j in range(0, WIDTH, MXU_DIM)], axis=1)
    tn = t * jax.lax.rsqrt(ms + EPS) * gain
    outs = []
    for j in range(WIDTH // LANES):
        c = tn[:, j * LANES:(j + 1) * LANES]
        outs.append(c * rc + pltpu.roll(c, LANES - ROPE_HALF, 1) * rs_next
                    + pltpu.roll(c, ROPE_HALF, 1) * rs_prev)
    return jnp.concatenate(outs, axis=1)


def _proj_kernel(x_ref, n1g_ref, win_ref, wgate_ref, bgate_ref, qg_ref, kg_ref,
                 rc_ref, rsn_ref, rsp_ref, gmean_ref, fc_ref,
                 q_ref, kt_ref, v_ref, glu_ref, zr_ref, zi_ref, gates_ref):
    x = x_ref[...]
    h = x * jax.lax.rsqrt(jnp.mean(x * x, axis=-1, keepdims=True) + EPS) * n1g_ref[...]
    hb = h.astype(BF16)

    def seg(lo, width):
        return jnp.dot(hb, win_ref[:, lo:lo + width], preferred_element_type=F32)

    rc, rsn, rsp = rc_ref[...], rsn_ref[...], rsp_ref[...]
    gmean = gmean_ref[...]
    q = _norm_rope(seg(0, WIDTH), qg_ref[...], gmean, rc, rsn, rsp)
    q_ref[...] = (q * (D_COMP ** -0.5 * LOG2_E)).astype(BF16)
    k = _norm_rope(seg(WIDTH, WIDTH), kg_ref[...], gmean, rc, rsn, rsp)
    kt_ref[0] = k.T.astype(BF16)
    v_ref[...] = seg(2 * WIDTH, WIDTH).astype(BF16)

    ga = seg(3 * WIDTH, WIDTH)
    gb = seg(4 * WIDTH, WIDTH)
    glu_ref[...] = ga * jax.nn.sigmoid(gb)

    f = seg(5 * WIDTH, WIDTH)
    fc = fc_ref[...]
    for j in range(WIDTH // GROUP_C):
        ab = jnp.dot(f[:, j * GROUP_C:(j + 1) * GROUP_C].astype(BF16), fc,
                     preferred_element_type=F32)
        zr_ref[:, j * GROUP_C:(j + 1) * GROUP_C] = ab[:, :GROUP_C]
        zi_ref[:, j * GROUP_C:(j + 1) * GROUP_C] = -ab[:, GROUP_C:]

    for j in range(3):
        lo = j * D_MODEL
        gt = jnp.dot(hb, wgate_ref[:, lo:lo + D_MODEL], preferred_element_type=F32)
        gates_ref[:, lo:lo + D_MODEL] = jax.nn.sigmoid(gt + bgate_ref[:, lo:lo + D_MODEL]).astype(BF16)


def _proj_call(x2, n1g, win, wgate, bgate, qg, kg, rc, rsn, rsp, gmean, fc, batch, seq):
    rows = x2.shape[0]
    tm = ROW_BLOCK
    blocks_per_seq = seq // tm
    row_spec = lambda w: pl.BlockSpec((tm, w), lambda i: (i, 0))
    tab_spec = pl.BlockSpec((tm, LANES), lambda i: (i % blocks_per_seq, 0))
    in_width = win.shape[1]
    return pl.pallas_call(
        _proj_kernel,
        grid=(rows // tm,),
        in_specs=[
            row_spec(D_MODEL),
            _const_spec((1, D_MODEL)),
            _const_spec((D_MODEL, in_width)),
            _const_spec((D_MODEL, 3 * D_MODEL)),
            _const_spec((1, 3 * D_MODEL)),
            _const_spec((1, WIDTH)),
            _const_spec((1, WIDTH)),
            tab_spec, tab_spec, tab_spec,
            _const_spec((MXU_DIM, MXU_DIM)),
            _const_spec((GROUP_C, 2 * GROUP_C)),
        ],
        out_specs=[
            row_spec(WIDTH),
            pl.BlockSpec((1, WIDTH, tm), lambda i: (i // blocks_per_seq, 0, i % blocks_per_seq)),
            row_spec(WIDTH),
            row_spec(WIDTH),
            row_spec(WIDTH),
            row_spec(WIDTH),
            row_spec(3 * D_MODEL),
        ],
        out_shape=[
            jax.ShapeDtypeStruct((rows, WIDTH), BF16),
            jax.ShapeDtypeStruct((batch, WIDTH, seq), BF16),
            jax.ShapeDtypeStruct((rows, WIDTH), BF16),
            jax.ShapeDtypeStruct((rows, WIDTH), F32),
            jax.ShapeDtypeStruct((rows, WIDTH), F32),
            jax.ShapeDtypeStruct((rows, WIDTH), F32),
            jax.ShapeDtypeStruct((rows, 3 * D_MODEL), BF16),
        ],
        compiler_params=pltpu.CompilerParams(
            dimension_semantics=("parallel",), vmem_limit_bytes=VMEM_LIMIT_BYTES),
        name="proj",
    )(x2, n1g, win, wgate, bgate, qg, kg, rc, rsn, rsp, gmean, fc)


def _attn_kernel(bound_ref, q_ref, kt_ref, v_ref, lq1_ref, lk1_ref, lq2_ref, lk2_ref, sg_ref,
                 o_ref, vext_ref, kaug_ref, col_ref, acc_ref, *, lam_init, seq):
    tq = Q_BLOCK
    lane = jax.lax.broadcasted_iota(jnp.int32, (1, LANES), 1)
    first_head = lane < D_V
    comp_lanes = lane < D_COMP
    bound = bound_ref[0]
    use_row_max = bound > SAFE_SCORE_BOUND * LOG2_E

    @pl.when(pl.program_id(2) == 0)
    def _prepare_keys_values():
        v = v_ref[0]
        one = jnp.ones_like(v)
        vext_ref[0] = jnp.where(first_head, v, one)
        vext_ref[1] = jnp.where(first_head, one, v)
        row = jax.lax.broadcasted_iota(jnp.int32, (K_AUG - D_COMP, seq), 0)
        extra = jnp.where(row == 0, jnp.where(use_row_max, 1.0, -bound), 0.0).astype(BF16)
        for c in range(4):
            kaug_ref[c, 0:D_COMP, :] = kt_ref[0, c * D_COMP:(c + 1) * D_COMP, :]
            kaug_ref[c, D_COMP:, :] = extra

    lam = (jnp.exp(jnp.sum(lq1_ref[...] * lk1_ref[...], axis=1, keepdims=True))
           - jnp.exp(jnp.sum(lq2_ref[...] * lk2_ref[...], axis=1, keepdims=True))
           + lam_init)
    qf = q_ref[0].astype(F32)

    def shifted_query(c):
        qc = qf if c == 0 else pltpu.roll(qf, LANES - c * D_COMP, 1)
        qc = jnp.where(comp_lanes, qc, 0.0)
        col_ref[...] = jnp.ones_like(col_ref)

        @pl.when(use_row_max)
        def _exact_row_max():
            q0 = qc[:, :K_AUG].astype(BF16)

            def mx(j, m):
                start = pl.multiple_of(j * K_SUB, K_SUB)
                s = jnp.dot(q0, kaug_ref[c, :, pl.ds(start, K_SUB)], preferred_element_type=F32)
                return jnp.maximum(m, jnp.max(s, axis=1, keepdims=True))
            m = jax.lax.fori_loop(0, seq // K_SUB, mx, jnp.full((tq, 1), -jnp.inf, F32))
            col_ref[...] = -m

        qa = jnp.where(lane == D_COMP, col_ref[...], qc)
        return qa[:, :K_AUG].astype(BF16)

    def softmax_pv(head):
        qa1 = shifted_query(2 * head)
        qa2 = shifted_query(2 * head + 1)
        acc_ref[...] = jnp.zeros_like(acc_ref)

        def body(j, carry):
            pv = None
            for u in range(K_CHUNK // K_SUB):
                start = pl.multiple_of(j * K_CHUNK + u * K_SUB, K_SUB)
                s1 = jnp.dot(qa1, kaug_ref[2 * head, :, pl.ds(start, K_SUB)],
                             preferred_element_type=F32)
                s2 = jnp.dot(qa2, kaug_ref[2 * head + 1, :, pl.ds(start, K_SUB)],
                             preferred_element_type=F32)
                p = jnp.concatenate([jnp.exp2(s1), jnp.exp2(s2)], axis=0).astype(BF16)
                d = jnp.dot(p, vext_ref[head, pl.ds(start, K_SUB), :], preferred_element_type=F32)
                pv = d if pv is None else pv + d
            acc_ref[...] += pv
            return carry

        jax.lax.fori_loop(0, seq // K_CHUNK, body, 0)
        return acc_ref[...]

    outs = []
    for head in range(2):
        valid = first_head if head == 0 else jnp.logical_not(first_head)
        acc = softmax_pv(head)
        a = acc / pltpu.roll(acc, D_V, 1)
        o = a[:tq] - lam * a[tq:]
        ss = jnp.sum(jnp.where(valid, o * o, 0.0), axis=1, keepdims=True) * (1.0 / D_V)
        outs.append(o * jax.lax.rsqrt(ss + EPS))
    y = jnp.where(first_head, outs[0], outs[1]) * sg_ref[...] * (1.0 - lam_init)
    o_ref[0] = y.astype(BF16)


def _attn_call(score_bound, q3, kt, v3, lq1, lk1, lq2, lk2, sg, lam_init):
    batch, seq, _ = q3.shape
    pairs = N_HEADS // 2
    vec = _const_spec((1, D_COMP))
    return pl.pallas_call(
        functools.partial(_attn_kernel, lam_init=lam_init, seq=seq),
        grid=(batch, pairs, seq // Q_BLOCK),
        in_specs=[
            pl.BlockSpec(memory_space=pltpu.SMEM),
            pl.BlockSpec((1, Q_BLOCK, LANES), lambda b, h, i: (b, i, h)),
            pl.BlockSpec((1, LANES, seq), lambda b, h, i: (b, h, 0)),
            pl.BlockSpec((1, seq, LANES), lambda b, h, i: (b, 0, h)),
            vec, vec, vec, vec,
            _const_spec((1, LANES)),
        ],
        out_specs=pl.BlockSpec((1, Q_BLOCK, LANES), lambda b, h, i: (b, i, h)),
        out_shape=jax.ShapeDtypeStruct((batch, seq, WIDTH), BF16),
        scratch_shapes=[
            pltpu.VMEM((2, seq, LANES), BF16),
            pltpu.VMEM((4, K_AUG, seq), BF16),
            pltpu.VMEM((Q_BLOCK, 1), F32),
            pltpu.VMEM((2 * Q_BLOCK, LANES), F32),
        ],
        compiler_params=pltpu.CompilerParams(
            dimension_semantics=("parallel", "parallel", "arbitrary"),
            vmem_limit_bytes=VMEM_LIMIT_BYTES),
        name="attn",
    )(score_bound, q3, kt, v3, lq1, lk1, lq2, lk2, sg)


def _seqdft_kernel(zr_ref, zi_ref, m1_ref, f2_ref, o_ref, t_ref):
    def stage1(n2, carry):
        rows = pl.ds(n2, DFT_N1, stride=DFT_N2)
        z = jnp.concatenate([zr_ref[0, rows, :], zi_ref[0, rows, :]], axis=0).astype(BF16)
        t = jnp.dot(m1_ref[n2], z, preferred_element_type=F32)
        t_ref[pl.ds(pl.multiple_of(n2 * 2 * DFT_N1, 2 * DFT_N1), 2 * DFT_N1), :] = t
        return carry

    jax.lax.fori_loop(0, DFT_N2, stage1, 0, unroll=4)
    f2 = f2_ref[...]

    def stage2(k1, carry):
        tr = t_ref[pl.ds(k1, DFT_N2, stride=2 * DFT_N1), :]
        ti = t_ref[pl.ds(k1 + DFT_N1, DFT_N2, stride=2 * DFT_N1), :]
        t = jnp.concatenate([tr, ti], axis=0).astype(BF16)
        y = jnp.dot(f2, t, preferred_element_type=F32)
        o_ref[0, pl.ds(k1, DFT_N2, stride=DFT_N1), :] = y
        return carry

    jax.lax.fori_loop(0, DFT_N1, stage2, 0, unroll=8)


def _seqdft_call(zr3, zi3, m1, f2):
    batch, seq, width = zr3.shape
    z_spec = pl.BlockSpec((1, seq, LANES), lambda b, c: (b, 0, c))
    return pl.pallas_call(
        _seqdft_kernel,
        grid=(batch, width // LANES),
        in_specs=[z_spec, z_spec,
                  _const_spec((DFT_N2, 2 * DFT_N1, 2 * DFT_N1)),
                  _const_spec((DFT_N2, 2 * DFT_N2))],
        out_specs=pl.BlockSpec((1, seq, LANES), lambda b, c: (b, 0, c)),
        out_shape=jax.ShapeDtypeStruct((batch, seq, width), F32),
        scratch_shapes=[pltpu.VMEM((DFT_N2 * 2 * DFT_N1, LANES), F32)],
        compiler_params=pltpu.CompilerParams(
            dimension_semantics=("parallel", "parallel"), vmem_limit_bytes=VMEM_LIMIT_BYTES),
        name="seqdft",
    )(zr3, zi3, m1, f2)


def _dft_tables(seq, group):
    assert seq == DFT_N1 * DFT_N2
    c = np.arange(group)
    ang = 2.0 * np.pi * ((c[:, None] * c[None, :]) % group) / group
    fc = np.concatenate([np.cos(ang), np.sin(ang)], axis=1)

    k1 = np.arange(DFT_N1)[None, :, None]
    n1 = np.arange(DFT_N1)[None, None, :]
    n2 = np.arange(DFT_N2)[:, None, None]
    th = 2.0 * np.pi * ((k1 * (DFT_N2 * n1 + n2)) % seq) / seq
    co, si = np.cos(th), np.sin(th)
    m1 = np.concatenate([np.concatenate([co, si], axis=2),
                         np.concatenate([-si, co], axis=2)], axis=1)

    k2 = np.arange(DFT_N2)
    ph = 2.0 * np.pi * ((k2[:, None] * k2[None, :]) % DFT_N2) / DFT_N2
    f2 = np.concatenate([np.cos(ph), np.sin(ph)], axis=1) / math.sqrt(seq * group)
    return tuple(jnp.asarray(t, dtype=F32).astype(BF16) for t in (fc, m1, f2))


def _merge_kernel(x_ref, att_ref, glu_ref, gprev_ref, gnext_ref, fr_ref, gates_ref,
                  cw_ref, cb_ref, lng_ref, lnb_ref, wpa_ref, wpb_ref, wpc_ref, wout_ref,
                  n2g_ref, wfi_ref, wfo_ref, o_ref, win_ref, cv_ref, *, blocks_per_seq):
    tm = ROW_BLOCK
    i = pl.program_id(0)
    pos = i % blocks_per_seq
    win_ref[0:HALO_ROWS, :] = jnp.where(pos == 0, 0.0, gprev_ref[...])
    win_ref[HALO_ROWS:HALO_ROWS + tm, :] = glu_ref[...]
    win_ref[HALO_ROWS + tm:, :] = jnp.where(pos == blocks_per_seq - 1, 0.0, gnext_ref[...])

    assert HALO_ROWS - CONV_PAD == 1
    rc = CONV_ROW_CHUNK
    for c0 in range(0, WIDTH, LANES):
        for r0 in range(0, tm, rc):
            acc = None
            for b in range(SUBLANES):
                part = None
                for a in range(-(-(CONV_WIDTH + 1) // SUBLANES)):
                    t = SUBLANES * a + b - 1
                    if 0 <= t < CONV_WIDTH:
                        lo = r0 + SUBLANES * a
                        term = (cw_ref[t:t + 1, c0:c0 + LANES]
                                * win_ref[lo:lo + rc + SUBLANES, c0:c0 + LANES])
                        part = term if part is None else part + term
                shifted = part[b:b + rc]
                acc = shifted if acc is None else acc + shifted
            cv_ref[r0:r0 + rc, c0:c0 + LANES] = acc

    cv = cv_ref[...] + cb_ref[...]
    mu = jnp.mean(cv, axis=-1, keepdims=True)
    d = cv - mu
    var = jnp.mean(d * d, axis=-1, keepdims=True)
    cv = d * jax.lax.rsqrt(var + EPS) * lng_ref[...] + lnb_ref[...]
    sb = (cv * jax.nn.sigmoid(cv)).astype(BF16)

    y_a = jnp.dot(att_ref[...], wpa_ref[...], preferred_element_type=F32)
    y_b = jnp.dot(sb, wpb_ref[...], preferred_element_type=F32)
    y_c = jnp.dot(fr_ref[...].astype(BF16), wpc_ref[...], preferred_element_type=F32)
    merged = (gates_ref[:, 0:D_MODEL].astype(F32) * y_a
              + gates_ref[:, D_MODEL:2 * D_MODEL].astype(F32) * y_b
              + gates_ref[:, 2 * D_MODEL:].astype(F32) * y_c)
    x1 = x_ref[...] + jnp.dot(merged.astype(BF16), wout_ref[...], preferred_element_type=F32)

    h2 = x1 * jax.lax.rsqrt(jnp.mean(x1 * x1, axis=-1, keepdims=True) + EPS) * n2g_ref[...]
    h2b = h2.astype(BF16)
    acc = jnp.zeros((tm, D_MODEL), F32)
    for j in range(N_FF_CHUNKS):
        gu = jnp.dot(h2b, wfi_ref[j], preferred_element_type=F32)
        gt = gu[:, :FF_CHUNK]
        a = (gt * jax.nn.sigmoid(gt) * gu[:, FF_CHUNK:]).astype(BF16)
        acc = acc + jnp.dot(a, wfo_ref[j], preferred_element_type=F32)
    o_ref[...] = x1 + acc


def _merge_call(x2, att, glu, fr, gates, cw, cb, lng, lnb, wpa, wpb, wpc, wout, n2g, wfi, wfo, seq):
    rows = x2.shape[0]
    tm = ROW_BLOCK
    blocks_per_seq = seq // tm
    halo_per_block = tm // HALO_ROWS
    n_halo_blocks = rows // HALO_ROWS
    row_spec = lambda w: pl.BlockSpec((tm, w), lambda i: (i, 0))
    prev_spec = pl.BlockSpec((HALO_ROWS, WIDTH),
                             lambda i: (jnp.maximum(i * halo_per_block - 1, 0), 0))
    next_spec = pl.BlockSpec((HALO_ROWS, WIDTH),
                             lambda i: (jnp.minimum((i + 1) * halo_per_block, n_halo_blocks - 1), 0))
    return pl.pallas_call(
        functools.partial(_merge_kernel, blocks_per_seq=blocks_per_seq),
        grid=(rows // tm,),
        in_specs=[
            row_spec(D_MODEL), row_spec(WIDTH), row_spec(WIDTH), prev_spec, next_spec,
            row_spec(WIDTH), row_spec(3 * D_MODEL),
            _const_spec((CONV_WIDTH + 1, WIDTH)),
            _const_spec((1, WIDTH)), _const_spec((1, WIDTH)), _const_spec((1, WIDTH)),
            _const_spec((WIDTH, D_MODEL)), _const_spec((WIDTH, D_MODEL)), _const_spec((WIDTH, D_MODEL)),
            _const_spec((D_MODEL, D_MODEL)),
            _const_spec((1, D_MODEL)),
            _const_spec((N_FF_CHUNKS, D_MODEL, 2 * FF_CHUNK)),
            _const_spec((N_FF_CHUNKS, FF_CHUNK, D_MODEL)),
        ],
        out_specs=row_spec(D_MODEL),
        out_shape=jax.ShapeDtypeStruct((rows, D_MODEL), F32),
        scratch_shapes=[pltpu.VMEM((tm + 2 * HALO_ROWS, WIDTH), F32),
                        pltpu.VMEM((tm, WIDTH), F32)],
        compiler_params=pltpu.CompilerParams(
            dimension_semantics=("parallel",), vmem_limit_bytes=VMEM_LIMIT_BYTES),
        name="merge_ffn",
    )(x2, att, glu, glu, glu, fr, gates, cw, cb, lng, lnb, wpa, wpb, wpc, wout, n2g, wfi, wfo)


def _rope_lane_tables(seq):
    pos = jnp.arange(seq, dtype=F32)
    inv = 1.0 / (ROPE_THETA ** (jnp.arange(0, ROPE_DIMS, 2, dtype=F32) / ROPE_DIMS))
    ang = pos[:, None] * inv[None, :]
    cos, sin = jnp.cos(ang), jnp.sin(ang)
    d = np.arange(LANES) % D_COMP
    f = d % ROPE_HALF
    rc = jnp.where(d[None, :] < ROPE_DIMS, cos[:, f], 1.0)
    rs_next = jnp.where(d[None, :] < ROPE_HALF, -sin[:, f], 0.0)
    rs_prev = jnp.where((d[None, :] >= ROPE_HALF) & (d[None, :] < ROPE_DIMS), sin[:, f], 0.0)
    return rc.astype(F32), rs_next.astype(F32), rs_prev.astype(F32)


def kernel(x, norm1_g, w_in, qnorm_g, knorm_g, lambda_q1, lambda_k1, lambda_q2, lambda_k2, subln_g,
           w_proj_a, conv_w, conv_b, conv_ln_g, conv_ln_b, w_proj_b, w_proj_c, w_gate, b_gate, w_out,
           norm2_g, w_ffn_in, w_ffn_out):
    batch, seq, d = x.shape
    depth = w_in.shape[0]
    rows = batch * seq
    rc, rsn, rsp = _rope_lane_tables(seq)
    fc, m1, f2 = _dft_tables(seq, GROUP_C)
    gidx = np.arange(MXU_DIM) // D_COMP
    gmean = jnp.asarray((gidx[:, None] == gidx[None, :]) / D_COMP, dtype=F32).astype(BF16)

    x2 = x.reshape(rows, d)
    for l in range(depth):
        lam_init = 0.8 - 0.6 * math.exp(-0.3 * l)
        row = lambda a: a[l].reshape(1, -1)
        q, kt, v, glu, zr, zi, gates = _proj_call(
            x2, row(norm1_g), w_in[l].astype(BF16), w_gate[l].astype(BF16), row(b_gate),
            jnp.tile(qnorm_g[l], WIDTH // D_COMP).reshape(1, WIDTH),
            jnp.tile(knorm_g[l], WIDTH // D_COMP).reshape(1, WIDTH),
            rc, rsn, rsp, gmean, fc, batch, seq)
        score_bound = (SCORE_BOUND_MARGIN * LOG2_E * math.sqrt(D_COMP) * jnp.max(jnp.abs(qnorm_g[l]))
                       * jnp.max(jnp.abs(knorm_g[l]))).reshape(1).astype(F32)
        att = _attn_call(
            score_bound, q.reshape(batch, seq, WIDTH), kt, v.reshape(batch, seq, WIDTH),
            row(lambda_q1), row(lambda_k1), row(lambda_q2), row(lambda_k2),
            jnp.tile(subln_g[l], LANES // D_V).reshape(1, LANES), lam_init)
        fr = _seqdft_call(zr.reshape(batch, seq, WIDTH), zi.reshape(batch, seq, WIDTH), m1, f2)
        wfi = w_ffn_in[l].astype(BF16)
        wfi = jnp.concatenate([wfi[:, :D_FF].reshape(d, N_FF_CHUNKS, FF_CHUNK),
                               wfi[:, D_FF:].reshape(d, N_FF_CHUNKS, FF_CHUNK)], axis=2)
        wfi = wfi.transpose(1, 0, 2)
        wfo = w_ffn_out[l].astype(BF16).reshape(N_FF_CHUNKS, FF_CHUNK, d)
        cw = jnp.concatenate([conv_w[l], jnp.zeros((1, WIDTH), F32)], axis=0)
        x2 = _merge_call(
            x2, att.reshape(rows, WIDTH), glu, fr.reshape(rows, WIDTH), gates,
            cw, row(conv_b), row(conv_ln_g), row(conv_ln_b),
            w_proj_a[l].astype(BF16), w_proj_b[l].astype(BF16), w_proj_c[l].astype(BF16),
            w_out[l].astype(BF16), row(norm2_g), wfi, wfo, seq)
    return x2.reshape(batch, seq, d)
```

```python
import functools
import math

import numpy as np
import jax
import jax.numpy as jnp
from jax.experimental import pallas as pl
from jax.experimental.pallas import tpu as pltpu

D_MODEL = 1024
N_HEADS = 8
D_COMP = 32
D_V = 64
WIDTH = 512
ROPE_DIMS = 8
ROPE_HALF = 4
ROPE_THETA = 500000.0
CONV_WIDTH = 31
CONV_PAD = 15
GROUP_C = 128
D_FF = 2816
FF_CHUNK = 256
N_FF_CHUNKS = D_FF // FF_CHUNK
EPS = 1e-6

LANES = 128
SUBLANES = 8
MXU_DIM = 256
VMEM_LIMIT_BYTES = 56 * 1024 * 1024

DFT_N1 = 128
DFT_N2 = 64

ROW_BLOCK = 512
HALO_ROWS = 16
CONV_ROW_CHUNK = 64
Q_BLOCK = 1024
K_SUB = 256
K_CHUNK = 2048
K_AUG = 64
K_F8 = 256
SAFE_SCORE_BOUND = 40.0
SCORE_BOUND_MARGIN = 1.02
LOG2_E = math.log2(math.e)

BF16 = jnp.bfloat16
F8 = jnp.float8_e4m3fn
F32 = jnp.float32


def _const_spec(shape):
    nd = len(shape)
    return pl.BlockSpec(shape, lambda *_: (0,) * nd, pipeline_mode=pl.Buffered(1))


def _split_dot(a_f32, b_bf16):
    hi = a_f32.astype(BF16)
    lo = (a_f32 - hi.astype(F32)).astype(BF16)
    return (jnp.dot(hi, b_bf16, preferred_element_type=F32)
            + jnp.dot(lo, b_bf16, preferred_element_type=F32))


def _norm_rope(t, gain, group_mean, rc, rs_next, rs_prev):
    sq = t * t
    ms = jnp.concatenate([_split_dot(sq[:, j:j + MXU_DIM], group_mean)
                          for j in range(0, WIDTH, MXU_DIM)], axis=1)
    tn = t * jax.lax.rsqrt(ms + EPS) * gain
    outs = []
    for j in range(WIDTH // LANES):
        c = tn[:, j * LANES:(j + 1) * LANES]
        outs.append(c * rc + pltpu.roll(c, LANES - ROPE_HALF, 1) * rs_next
                    + pltpu.roll(c, ROPE_HALF, 1) * rs_prev)
    return jnp.concatenate(outs, axis=1)


def _proj_kernel(x_ref, n1g_ref, win_ref, wgate_ref, bgate_ref, qg_ref, kg_ref,
                 rc_ref, rsn_ref, rsp_ref, gmean_ref, fc_ref,
                 q_ref, kt_ref, v_ref, glu_ref, zr_ref, zi_ref, gates_ref):
    x = x_ref[...]
    h = x * jax.lax.rsqrt(jnp.mean(x * x, axis=-1, keepdims=True) + EPS) * n1g_ref[...]
    hb = h.astype(BF16)

    def seg(lo, width):
        return jnp.dot(hb, win_ref[:, lo:lo + width], preferred_element_type=F32)

    rc, rsn, rsp = rc_ref[...], rsn_ref[...], rsp_ref[...]
    gmean = gmean_ref[...]
    q = _norm_rope(seg(0, WIDTH), qg_ref[...], gmean, rc, rsn, rsp)
    q_ref[...] = (q * (D_COMP ** -0.5 * LOG2_E)).astype(BF16)
    k = _norm_rope(seg(WIDTH, WIDTH), kg_ref[...], gmean, rc, rsn, rsp)
    kt_ref[0] = k.T.astype(BF16)
    v_ref[...] = seg(2 * WIDTH, WIDTH).astype(BF16)

    ga = seg(3 * WIDTH, WIDTH)
    gb = seg(4 * WIDTH, WIDTH)
    glu_ref[...] = ga * jax.nn.sigmoid(gb)

    f = seg(5 * WIDTH, WIDTH)
    fc = fc_ref[...]
    for j in range(WIDTH // GROUP_C):
        ab = jnp.dot(f[:, j * GROUP_C:(j + 1) * GROUP_C].astype(BF16), fc,
                     preferred_element_type=F32)
        zr_ref[:, j * GROUP_C:(j + 1) * GROUP_C] = ab[:, :GROUP_C]
        zi_ref[:, j * GROUP_C:(j + 1) * GROUP_C] = -ab[:, GROUP_C:]

    for j in range(3):
        lo = j * D_MODEL
        gt = jnp.dot(hb, wgate_ref[:, lo:lo + D_MODEL], preferred_element_type=F32)
        gates_ref[:, lo:lo + D_MODEL] = jax.nn.sigmoid(gt + bgate_ref[:, lo:lo + D_MODEL]).astype(BF16)


def _proj_call(x2, n1g, win, wgate, bgate, qg, kg, rc, rsn, rsp, gmean, fc, batch, seq):
    rows = x2.shape[0]
    tm = ROW_BLOCK
    blocks_per_seq = seq // tm
    row_spec = lambda w: pl.BlockSpec((tm, w), lambda i: (i, 0))
    tab_spec = pl.BlockSpec((tm, LANES), lambda i: (i % blocks_per_seq, 0))
    in_width = win.shape[1]
    return pl.pallas_call(
        _proj_kernel,
        grid=(rows // tm,),
        in_specs=[
            row_spec(D_MODEL),
            _const_spec((1, D_MODEL)),
            _const_spec((D_MODEL, in_width)),
            _const_spec((D_MODEL, 3 * D_MODEL)),
            _const_spec((1, 3 * D_MODEL)),
            _const_spec((1, WIDTH)),
            _const_spec((1, WIDTH)),
            tab_spec, tab_spec, tab_spec,
            _const_spec((MXU_DIM, MXU_DIM)),
            _const_spec((GROUP_C, 2 * GROUP_C)),
        ],
        out_specs=[
            row_spec(WIDTH),
            pl.BlockSpec((1, WIDTH, tm), lambda i: (i // blocks_per_seq, 0, i % blocks_per_seq)),
            row_spec(WIDTH),
            row_spec(WIDTH),
            row_spec(WIDTH),
            row_spec(WIDTH),
            row_spec(3 * D_MODEL),
        ],
        out_shape=[
            jax.ShapeDtypeStruct((rows, WIDTH), BF16),
            jax.ShapeDtypeStruct((batch, WIDTH, seq), BF16),
            jax.ShapeDtypeStruct((rows, WIDTH), BF16),
            jax.ShapeDtypeStruct((rows, WIDTH), F32),
            jax.ShapeDtypeStruct((rows, WIDTH), F32),
            jax.ShapeDtypeStruct((rows, WIDTH), F32),
            jax.ShapeDtypeStruct((rows, 3 * D_MODEL), BF16),
        ],
        compiler_params=pltpu.CompilerParams(
            dimension_semantics=("parallel",), vmem_limit_bytes=VMEM_LIMIT_BYTES),
        name="proj",
    )(x2, n1g, win, wgate, bgate, qg, kg, rc, rsn, rsp, gmean, fc)


def _split3(x):
    a = x.astype(F8).astype(F32)
    r = x - a
    b = (r * 16.0).astype(F8).astype(F32)
    r = r - b * 0.0625
    c = (r * 256.0).astype(F8).astype(F32)
    return a, b, c


def _attn_kernel(bound_ref, q_ref, kt_ref, v_ref, lq1_ref, lk1_ref, lq2_ref, lk2_ref, sg_ref,
                 o_ref, vext_ref, k8_ref, kaug_ref, col_ref, acc_ref, *, lam_init, seq):
    tq = Q_BLOCK
    lane = jax.lax.broadcasted_iota(jnp.int32, (1, LANES), 1)
    first_head = lane < D_V
    comp_lanes = lane < D_COMP
    use_row_max = bound_ref[0] > SAFE_SCORE_BOUND * LOG2_E
    use_bound = jnp.logical_not(use_row_max)

    @pl.when(pl.program_id(2) == 0)
    def _prepare_keys_values():
        v = v_ref[0]
        one = jnp.ones_like(v)
        vext_ref[0] = jnp.where(first_head, v, one)
        vext_ref[1] = jnp.where(first_head, one, v)

        @pl.when(use_row_max)
        def _bf16_keys():
            row = jax.lax.broadcasted_iota(jnp.int32, (K_AUG - D_COMP, seq), 0)
            extra = jnp.where(row == 0, 1.0, 0.0).astype(BF16)
            for c in range(4):
                kaug_ref[c, 0:D_COMP, :] = kt_ref[0, c * D_COMP:(c + 1) * D_COMP, :]
                kaug_ref[c, D_COMP:, :] = extra

        @pl.when(use_bound)
        def _fp8_keys():
            for c in range(4):
                a, b, c3 = _split3(kt_ref[0, c * D_COMP:(c + 1) * D_COMP, :].astype(F32))
                blocks = (a, a * 0.25, b * 0.25, b * 0.0625, a * 0.0625, c3 * 0.0625)
                for j, blk in enumerate(blocks):
                    k8_ref[c, j * D_COMP:(j + 1) * D_COMP, :] = blk.astype(F8)
                k8_ref[c, len(blocks) * D_COMP:, :] = jnp.zeros((K_F8 - len(blocks) * D_COMP, seq), F8)

    lam = (jnp.exp(jnp.sum(lq1_ref[...] * lk1_ref[...], axis=1, keepdims=True))
           - jnp.exp(jnp.sum(lq2_ref[...] * lk2_ref[...], axis=1, keepdims=True))
           + lam_init)

    def component(c):
        qf = q_ref[0].astype(F32)
        qc = qf if c == 0 else pltpu.roll(qf, LANES - c * D_COMP, 1)
        return jnp.where(comp_lanes, qc, 0.0)

    def fp8_query_maker():
        a, b, c3 = _split3(q_ref[0].astype(F32))
        a4, a16, b4, b16, c16 = a * 0.25, a * 0.0625, b * 0.25, b * 0.0625, c3 * 0.0625
        block = lane // D_COMP

        def query_fp8(c):
            def at(x, j):
                shift = ((j - c) % (LANES // D_COMP)) * D_COMP
                return x if shift == 0 else pltpu.roll(x, shift, 1)
            lo = jnp.where(block == 0, at(a, 0),
                           jnp.where(block == 1, at(b4, 1),
                                     jnp.where(block == 2, at(a4, 2), at(b16, 3))))
            hi = jnp.where(block == 0, at(c16, 0), jnp.where(block == 1, at(a16, 1), 0.0))
            return jnp.concatenate([lo, hi], axis=1).astype(F8)
        return query_fp8

    def query_bf16(c):
        qc = component(c)
        q0 = qc[:, :K_AUG].astype(BF16)

        def mx(j, m):
            start = pl.multiple_of(j * K_SUB, K_SUB)
            s = jnp.dot(q0, kaug_ref[c, :, pl.ds(start, K_SUB)], preferred_element_type=F32)
            return jnp.maximum(m, jnp.max(s, axis=1, keepdims=True))
        col_ref[...] = -jax.lax.fori_loop(0, seq // K_SUB, mx, jnp.full((tq, 1), -jnp.inf, F32))
        qa = jnp.where(lane == D_COMP, col_ref[...], qc)
        return qa[:, :K_AUG].astype(BF16)

    def softmax_pv(head, qa1, qa2, k_ref, exp2_fn):
        acc_ref[...] = jnp.zeros_like(acc_ref)

        def body(j, carry):
            pv = None
            for u in range(K_CHUNK // K_SUB):
                start = pl.multiple_of(j * K_CHUNK + u * K_SUB, K_SUB)
                s1 = jnp.dot(qa1, k_ref[2 * head, :, pl.ds(start, K_SUB)],
                             preferred_element_type=F32)
                s2 = jnp.dot(qa2, k_ref[2 * head + 1, :, pl.ds(start, K_SUB)],
                             preferred_element_type=F32)
                p = jnp.concatenate([exp2_fn(s1), exp2_fn(s2)], axis=0).astype(BF16)
                d = jnp.dot(p, vext_ref[head, pl.ds(start, K_SUB), :], preferred_element_type=F32)
                pv = d if pv is None else pv + d
            acc_ref[...] += pv
            return carry

        jax.lax.fori_loop(0, seq // K_CHUNK, body, 0)
        return acc_ref[...]

    def attend(make_query, k_ref, exp2_fn):
        outs = []
        for head in range(2):
            valid = first_head if head == 0 else jnp.logical_not(first_head)
            acc = softmax_pv(head, make_query(2 * head), make_query(2 * head + 1), k_ref, exp2_fn)
            a = acc / pltpu.roll(acc, D_V, 1)
            o = a[:tq] - lam * a[tq:]
            ss = jnp.sum(jnp.where(valid, o * o, 0.0), axis=1, keepdims=True) * (1.0 / D_V)
            outs.append(o * jax.lax.rsqrt(ss + EPS))
        y = jnp.where(first_head, outs[0], outs[1]) * sg_ref[...] * (1.0 - lam_init)
        o_ref[0] = y.astype(BF16)

    @pl.when(use_bound)
    def _fp8_scores():
        attend(fp8_query_maker(), k8_ref, lambda s: jnp.exp2(s.astype(BF16)))

    @pl.when(use_row_max)
    def _bf16_scores():
        attend(query_bf16, kaug_ref, jnp.exp2)


def _attn_call(score_bound, q3, kt, v3, lq1, lk1, lq2, lk2, sg, lam_init):
    batch, seq, _ = q3.shape
    pairs = N_HEADS // 2
    vec = _const_spec((1, D_COMP))
    return pl.pallas_call(
        functools.partial(_attn_kernel, lam_init=lam_init, seq=seq),
        grid=(batch, pairs, seq // Q_BLOCK),
        in_specs=[
            pl.BlockSpec(memory_space=pltpu.SMEM),
            pl.BlockSpec((1, Q_BLOCK, LANES), lambda b, h, i: (b, i, h)),
            pl.BlockSpec((1, LANES, seq), lambda b, h, i: (b, h, 0)),
            pl.BlockSpec((1, seq, LANES), lambda b, h, i: (b, 0, h)),
            vec, vec, vec, vec,
            _const_spec((1, LANES)),
        ],
        out_specs=pl.BlockSpec((1, Q_BLOCK, LANES), lambda b, h, i: (b, i, h)),
        out_shape=jax.ShapeDtypeStruct((batch, seq, WIDTH), BF16),
        scratch_shapes=[
            pltpu.VMEM((2, seq, LANES), BF16),
            pltpu.VMEM((4, K_F8, seq), F8),
            pltpu.VMEM((4, K_AUG, seq), BF16),
            pltpu.VMEM((Q_BLOCK, 1), F32),
            pltpu.VMEM((2 * Q_BLOCK, LANES), F32),
        ],
        compiler_params=pltpu.CompilerParams(
            dimension_semantics=("parallel", "parallel", "arbitrary"),
            vmem_limit_bytes=VMEM_LIMIT_BYTES),
        name="attn",
    )(score_bound, q3, kt, v3, lq1, lk1, lq2, lk2, sg)


def _seqdft_kernel(zr_ref, zi_ref, m1_ref, f2_ref, o_ref, t_ref):
    def stage1(n2, carry):
        rows = pl.ds(n2, DFT_N1, stride=DFT_N2)
        z = jnp.concatenate([zr_ref[0, rows, :], zi_ref[0, rows, :]], axis=0).astype(BF16)
        t = jnp.dot(m1_ref[n2], z, preferred_element_type=F32)
        t_ref[pl.ds(pl.multiple_of(n2 * 2 * DFT_N1, 2 * DFT_N1), 2 * DFT_N1), :] = t
        return carry

    jax.lax.fori_loop(0, DFT_N2, stage1, 0, unroll=4)
    f2 = f2_ref[...]

    def stage2(k1, carry):
        tr = t_ref[pl.ds(k1, DFT_N2, stride=2 * DFT_N1), :]
        ti = t_ref[pl.ds(k1 + DFT_N1, DFT_N2, stride=2 * DFT_N1), :]
        t = jnp.concatenate([tr, ti], axis=0).astype(BF16)
        y = jnp.dot(f2, t, preferred_element_type=F32)
        o_ref[0, pl.ds(k1, DFT_N2, stride=DFT_N1), :] = y
        return carry

    jax.lax.fori_loop(0, DFT_N1, stage2, 0, unroll=8)


def _seqdft_call(zr3, zi3, m1, f2):
    batch, seq, width = zr3.shape
    z_spec = pl.BlockSpec((1, seq, LANES), lambda b, c: (b, 0, c))
    return pl.pallas_call(
        _seqdft_kernel,
        grid=(batch, width // LANES),
        in_specs=[z_spec, z_spec,
                  _const_spec((DFT_N2, 2 * DFT_N1, 2 * DFT_N1)),
                  _const_spec((DFT_N2, 2 * DFT_N2))],
        out_specs=pl.BlockSpec((1, seq, LANES), lambda b, c: (b, 0, c)),
        out_shape=jax.ShapeDtypeStruct((batch, seq, width), F32),
        scratch_shapes=[pltpu.VMEM((DFT_N2 * 2 * DFT_N1, LANES), F32)],
        compiler_params=pltpu.CompilerParams(
            dimension_semantics=("parallel", "parallel"), vmem_limit_bytes=VMEM_LIMIT_BYTES),
        name="seqdft",
    )(zr3, zi3, m1, f2)


def _dft_tables(seq, group):
    assert seq == DFT_N1 * DFT_N2
    c = np.arange(group)
    ang = 2.0 * np.pi * ((c[:, None] * c[None, :]) % group) / group
    fc = np.concatenate([np.cos(ang), np.sin(ang)], axis=1)

    k1 = np.arange(DFT_N1)[None, :, None]
    n1 = np.arange(DFT_N1)[None, None, :]
    n2 = np.arange(DFT_N2)[:, None, None]
    th = 2.0 * np.pi * ((k1 * (DFT_N2 * n1 + n2)) % seq) / seq
    co, si = np.cos(th), np.sin(th)
    m1 = np.concatenate([np.concatenate([co, si], axis=2),
                         np.concatenate([-si, co], axis=2)], axis=1)

    k2 = np.arange(DFT_N2)
    ph = 2.0 * np.pi * ((k2[:, None] * k2[None, :]) % DFT_N2) / DFT_N2
    f2 = np.concatenate([np.cos(ph), np.sin(ph)], axis=1) / math.sqrt(seq * group)
    return tuple(jnp.asarray(t, dtype=F32).astype(BF16) for t in (fc, m1, f2))


def _merge_kernel(x_ref, att_ref, glu_ref, gprev_ref, gnext_ref, fr_ref, gates_ref,
                  cw_ref, cb_ref, lng_ref, lnb_ref, wpa_ref, wpb_ref, wpc_ref, wout_ref,
                  n2g_ref, wfi_ref, wfo_ref, o_ref, win_ref, cv_ref, *, blocks_per_seq):
    tm = ROW_BLOCK
    i = pl.program_id(0)
    pos = i % blocks_per_seq
    win_ref[0:HALO_ROWS, :] = jnp.where(pos == 0, 0.0, gprev_ref[...])
    win_ref[HALO_ROWS:HALO_ROWS + tm, :] = glu_ref[...]
    win_ref[HALO_ROWS + tm:, :] = jnp.where(pos == blocks_per_seq - 1, 0.0, gnext_ref[...])

    assert HALO_ROWS - CONV_PAD == 1
    rc = CONV_ROW_CHUNK
    for c0 in range(0, WIDTH, LANES):
        for r0 in range(0, tm, rc):
            acc = None
            for b in range(SUBLANES):
                part = None
                for a in range(-(-(CONV_WIDTH + 1) // SUBLANES)):
                    t = SUBLANES * a + b - 1
                    if 0 <= t < CONV_WIDTH:
                        lo = r0 + SUBLANES * a
                        term = (cw_ref[t:t + 1, c0:c0 + LANES]
                                * win_ref[lo:lo + rc + SUBLANES, c0:c0 + LANES])
                        part = term if part is None else part + term
                shifted = part[b:b + rc]
                acc = shifted if acc is None else acc + shifted
            cv_ref[r0:r0 + rc, c0:c0 + LANES] = acc

    cv = cv_ref[...] + cb_ref[...]
    mu = jnp.mean(cv, axis=-1, keepdims=True)
    d = cv - mu
    var = jnp.mean(d * d, axis=-1, keepdims=True)
    cv = d * jax.lax.rsqrt(var + EPS) * lng_ref[...] + lnb_ref[...]
    sb = (cv * jax.nn.sigmoid(cv)).astype(BF16)

    y_a = jnp.dot(att_ref[...], wpa_ref[...], preferred_element_type=F32)
    y_b = jnp.dot(sb, wpb_ref[...], preferred_element_type=F32)
    y_c = jnp.dot(fr_ref[...].astype(BF16), wpc_ref[...], preferred_element_type=F32)
    merged = (gates_ref[:, 0:D_MODEL].astype(F32) * y_a
              + gates_ref[:, D_MODEL:2 * D_MODEL].astype(F32) * y_b
              + gates_ref[:, 2 * D_MODEL:].astype(F32) * y_c)
    x1 = x_ref[...] + jnp.dot(merged.astype(BF16), wout_ref[...], preferred_element_type=F32)

    h2 = x1 * jax.lax.rsqrt(jnp.mean(x1 * x1, axis=-1, keepdims=True) + EPS) * n2g_ref[...]
    h2b = h2.astype(BF16)
    acc = jnp.zeros((tm, D_MODEL), F32)
    for j in range(N_FF_CHUNKS):
        gu = jnp.dot(h2b, wfi_ref[j], preferred_element_type=F32)
        gt = gu[:, :FF_CHUNK]
        a = (gt * jax.nn.sigmoid(gt) * gu[:, FF_CHUNK:]).astype(BF16)
        acc = acc + jnp.dot(a, wfo_ref[j], preferred_element_type=F32)
    o_ref[...] = x1 + acc


def _merge_call(x2, att, glu, fr, gates, cw, cb, lng, lnb, wpa, wpb, wpc, wout, n2g, wfi, wfo, seq):
    rows = x2.shape[0]
    tm = ROW_BLOCK
    blocks_per_seq = seq // tm
    halo_per_block = tm // HALO_ROWS
    n_halo_blocks = rows // HALO_ROWS
    row_spec = lambda w: pl.BlockSpec((tm, w), lambda i: (i, 0))
    prev_spec = pl.BlockSpec((HALO_ROWS, WIDTH),
                             lambda i: (jnp.maximum(i * halo_per_block - 1, 0), 0))
    next_spec = pl.BlockSpec((HALO_ROWS, WIDTH),
                             lambda i: (jnp.minimum((i + 1) * halo_per_block, n_halo_blocks - 1), 0))
    return pl.pallas_call(
        functools.partial(_merge_kernel, blocks_per_seq=blocks_per_seq),
        grid=(rows // tm,),
        in_specs=[
            row_spec(D_MODEL), row_spec(WIDTH), row_spec(WIDTH), prev_spec, next_spec,
            row_spec(WIDTH), row_spec(3 * D_MODEL),
            _const_spec((CONV_WIDTH + 1, WIDTH)),
            _const_spec((1, WIDTH)), _const_spec((1, WIDTH)), _const_spec((1, WIDTH)),
            _const_spec((WIDTH, D_MODEL)), _const_spec((WIDTH, D_MODEL)), _const_spec((WIDTH, D_MODEL)),
            _const_spec((D_MODEL, D_MODEL)),
            _const_spec((1, D_MODEL)),
            _const_spec((N_FF_CHUNKS, D_MODEL, 2 * FF_CHUNK)),
            _const_spec((N_FF_CHUNKS, FF_CHUNK, D_MODEL)),
        ],
        out_specs=row_spec(D_MODEL),
        out_shape=jax.ShapeDtypeStruct((rows, D_MODEL), F32),
        scratch_shapes=[pltpu.VMEM((tm + 2 * HALO_ROWS, WIDTH), F32),
                        pltpu.VMEM((tm, WIDTH), F32)],
        compiler_params=pltpu.CompilerParams(
            dimension_semantics=("parallel",), vmem_limit_bytes=VMEM_LIMIT_BYTES),
        name="merge_ffn",
    )(x2, att, glu, glu, glu, fr, gates, cw, cb, lng, lnb, wpa, wpb, wpc, wout, n2g, wfi, wfo)


def _rope_lane_tables(seq):
    pos = jnp.arange(seq, dtype=F32)
    inv = 1.0 / (ROPE_THETA ** (jnp.arange(0, ROPE_DIMS, 2, dtype=F32) / ROPE_DIMS))
    ang = pos[:, None] * inv[None, :]
    cos, sin = jnp.cos(ang), jnp.sin(ang)
    d = np.arange(LANES) % D_COMP
    f = d % ROPE_HALF
    rc = jnp.where(d[None, :] < ROPE_DIMS, cos[:, f], 1.0)
    rs_next = jnp.where(d[None, :] < ROPE_HALF, -sin[:, f], 0.0)
    rs_prev = jnp.where((d[None, :] >= ROPE_HALF) & (d[None, :] < ROPE_DIMS), sin[:, f], 0.0)
    return rc.astype(F32), rs_next.astype(F32), rs_prev.astype(F32)


def kernel(x, norm1_g, w_in, qnorm_g, knorm_g, lambda_q1, lambda_k1, lambda_q2, lambda_k2, subln_g,
           w_proj_a, conv_w, conv_b, conv_ln_g, conv_ln_b, w_proj_b, w_proj_c, w_gate, b_gate, w_out,
           norm2_g, w_ffn_in, w_ffn_out):
    batch, seq, d = x.shape
    depth = w_in.shape[0]
    rows = batch * seq
    rc, rsn, rsp = _rope_lane_tables(seq)
    fc, m1, f2 = _dft_tables(seq, GROUP_C)
    gidx = np.arange(MXU_DIM) // D_COMP
    gmean = jnp.asarray((gidx[:, None] == gidx[None, :]) / D_COMP, dtype=F32).astype(BF16)

    x2 = x.reshape(rows, d)
    for l in range(depth):
        lam_init = 0.8 - 0.6 * math.exp(-0.3 * l)
        row = lambda a: a[l].reshape(1, -1)
        q, kt, v, glu, zr, zi, gates = _proj_call(
            x2, row(norm1_g), w_in[l].astype(BF16), w_gate[l].astype(BF16), row(b_gate),
            jnp.tile(qnorm_g[l], WIDTH // D_COMP).reshape(1, WIDTH),
            jnp.tile(knorm_g[l], WIDTH // D_COMP).reshape(1, WIDTH),
            rc, rsn, rsp, gmean, fc, batch, seq)
        score_bound = (SCORE_BOUND_MARGIN * LOG2_E * math.sqrt(D_COMP) * jnp.max(jnp.abs(qnorm_g[l]))
                       * jnp.max(jnp.abs(knorm_g[l]))).reshape(1).astype(F32)
        att = _attn_call(
            score_bound, q.reshape(batch, seq, WIDTH), kt, v.reshape(batch, seq, WIDTH),
            row(lambda_q1), row(lambda_k1), row(lambda_q2), row(lambda_k2),
            jnp.tile(subln_g[l], LANES // D_V).reshape(1, LANES), lam_init)
        fr = _seqdft_call(zr.reshape(batch, seq, WIDTH), zi.reshape(batch, seq, WIDTH), m1, f2)
        wfi = w_ffn_in[l].astype(BF16)
        wfi = jnp.concatenate([wfi[:, :D_FF].reshape(d, N_FF_CHUNKS, FF_CHUNK),
                               wfi[:, D_FF:].reshape(d, N_FF_CHUNKS, FF_CHUNK)], axis=2)
        wfi = wfi.transpose(1, 0, 2)
        wfo = w_ffn_out[l].astype(BF16).reshape(N_FF_CHUNKS, FF_CHUNK, d)
        cw = jnp.concatenate([conv_w[l], jnp.zeros((1, WIDTH), F32)], axis=0)
        x2 = _merge_call(
            x2, att.reshape(rows, WIDTH), glu, fr.reshape(rows, WIDTH), gates,
            cw, row(conv_b), row(conv_ln_g), row(conv_ln_b),
            w_proj_a[l].astype(BF16), w_proj_b[l].astype(BF16), w_proj_c[l].astype(BF16),
            w_out[l].astype(BF16), row(norm2_g), wfi, wfo, seq)
    return x2.reshape(batch, seq, d)
```

```python
import functools
import math

import numpy as np
import jax
import jax.numpy as jnp
from jax.experimental import pallas as pl
from jax.experimental.pallas import tpu as pltpu

D_MODEL = 1024
N_HEADS = 8
D_COMP = 32
D_V = 64
WIDTH = 512
ROPE_DIMS = 8
ROPE_HALF = 4
ROPE_THETA = 500000.0
CONV_WIDTH = 31
CONV_PAD = 15
GROUP_C = 128
D_FF = 2816
FF_CHUNK = 256
N_FF_CHUNKS = D_FF // FF_CHUNK
EPS = 1e-6

LANES = 128
SUBLANES = 8
MXU_DIM = 256
VMEM_LIMIT_BYTES = 56 * 1024 * 1024

DFT_N1 = 128
DFT_N2 = 64

ROW_BLOCK = 512
HALO_ROWS = 16
CONV_ROW_CHUNK = 64
Q_BLOCK = 2048
K_SUB = 256
K_CHUNK = 2048
K_AUG = 64
K_F8 = 256
SAFE_SCORE_BOUND = 40.0
SCORE_BOUND_MARGIN = 1.02
LOG2_E = math.log2(math.e)

BF16 = jnp.bfloat16
F8 = jnp.float8_e4m3fn
F32 = jnp.float32


def _const_spec(shape):
    nd = len(shape)
    return pl.BlockSpec(shape, lambda *_: (0,) * nd, pipeline_mode=pl.Buffered(1))


def _split_dot(a_f32, b_bf16):
    hi = a_f32.astype(BF16)
    lo = (a_f32 - hi.astype(F32)).astype(BF16)
    return (jnp.dot(hi, b_bf16, preferred_element_type=F32)
            + jnp.dot(lo, b_bf16, preferred_element_type=F32))


def _norm_rope(t, gain, group_mean, rc, rs_next, rs_prev):
    sq = t * t
    ms = jnp.concatenate([_split_dot(sq[:, j:j + MXU_DIM], group_mean)
                          for j in range(0, WIDTH, MXU_DIM)], axis=1)
    tn = t * jax.lax.rsqrt(ms + EPS) * gain
    outs = []
    for j in range(WIDTH // LANES):
        c = tn[:, j * LANES:(j + 1) * LANES]
        outs.append(c * rc + pltpu.roll(c, LANES - ROPE_HALF, 1) * rs_next
                    + pltpu.roll(c, ROPE_HALF, 1) * rs_prev)
    return jnp.concatenate(outs, axis=1)


def _proj_kernel(x_ref, n1g_ref, win_ref, wgate_ref, bgate_ref, qg_ref, kg_ref,
                 rc_ref, rsn_ref, rsp_ref, gmean_ref, fc_ref,
                 q_ref, kt_ref, v_ref, glu_ref, zr_ref, zi_ref, gates_ref):
    x = x_ref[...]
    h = x * jax.lax.rsqrt(jnp.mean(x * x, axis=-1, keepdims=True) + EPS) * n1g_ref[...]
    hb = h.astype(BF16)

    def seg(lo, width):
        return jnp.dot(hb, win_ref[:, lo:lo + width], preferred_element_type=F32)

    rc, rsn, rsp = rc_ref[...], rsn_ref[...], rsp_ref[...]
    gmean = gmean_ref[...]
    q = _norm_rope(seg(0, WIDTH), qg_ref[...], gmean, rc, rsn, rsp)
    q_ref[...] = (q * (D_COMP ** -0.5 * LOG2_E)).astype(BF16)
    k = _norm_rope(seg(WIDTH, WIDTH), kg_ref[...], gmean, rc, rsn, rsp)
    kt_ref[0] = k.T.astype(BF16)
    v_ref[...] = seg(2 * WIDTH, WIDTH).astype(BF16)

    ga = seg(3 * WIDTH, WIDTH)
    gb = seg(4 * WIDTH, WIDTH)
    glu_ref[...] = ga * jax.nn.sigmoid(gb)

    f = seg(5 * WIDTH, WIDTH)
    fc = fc_ref[...]
    for j in range(WIDTH // GROUP_C):
        ab = jnp.dot(f[:, j * GROUP_C:(j + 1) * GROUP_C].astype(BF16), fc,
                     preferred_element_type=F32)
        zr_ref[:, j * GROUP_C:(j + 1) * GROUP_C] = ab[:, :GROUP_C]
        zi_ref[:, j * GROUP_C:(j + 1) * GROUP_C] = -ab[:, GROUP_C:]

    for j in range(3):
        lo = j * D_MODEL
        gt = jnp.dot(hb, wgate_ref[:, lo:lo + D_MODEL], preferred_element_type=F32)
        gates_ref[:, lo:lo + D_MODEL] = jax.nn.sigmoid(gt + bgate_ref[:, lo:lo + D_MODEL]).astype(BF16)


def _proj_call(x2, n1g, win, wgate, bgate, qg, kg, rc, rsn, rsp, gmean, fc, batch, seq):
    rows = x2.shape[0]
    tm = ROW_BLOCK
    blocks_per_seq = seq // tm
    row_spec = lambda w: pl.BlockSpec((tm, w), lambda i: (i, 0))
    tab_spec = pl.BlockSpec((tm, LANES), lambda i: (i % blocks_per_seq, 0))
    in_width = win.shape[1]
    return pl.pallas_call(
        _proj_kernel,
        grid=(rows // tm,),
        in_specs=[
            row_spec(D_MODEL),
            _const_spec((1, D_MODEL)),
            _const_spec((D_MODEL, in_width)),
            _const_spec((D_MODEL, 3 * D_MODEL)),
            _const_spec((1, 3 * D_MODEL)),
            _const_spec((1, WIDTH)),
            _const_spec((1, WIDTH)),
            tab_spec, tab_spec, tab_spec,
            _const_spec((MXU_DIM, MXU_DIM)),
            _const_spec((GROUP_C, 2 * GROUP_C)),
        ],
        out_specs=[
            row_spec(WIDTH),
            pl.BlockSpec((1, WIDTH, tm), lambda i: (i // blocks_per_seq, 0, i % blocks_per_seq)),
            row_spec(WIDTH),
            row_spec(WIDTH),
            row_spec(WIDTH),
            row_spec(WIDTH),
            row_spec(3 * D_MODEL),
        ],
        out_shape=[
            jax.ShapeDtypeStruct((rows, WIDTH), BF16),
            jax.ShapeDtypeStruct((batch, WIDTH, seq), BF16),
            jax.ShapeDtypeStruct((rows, WIDTH), BF16),
            jax.ShapeDtypeStruct((rows, WIDTH), F32),
            jax.ShapeDtypeStruct((rows, WIDTH), F32),
            jax.ShapeDtypeStruct((rows, WIDTH), F32),
            jax.ShapeDtypeStruct((rows, 3 * D_MODEL), BF16),
        ],
        compiler_params=pltpu.CompilerParams(
            dimension_semantics=("parallel",), vmem_limit_bytes=VMEM_LIMIT_BYTES),
        name="proj",
    )(x2, n1g, win, wgate, bgate, qg, kg, rc, rsn, rsp, gmean, fc)


def _split3(x):
    a = x.astype(F8).astype(F32)
    r = x - a
    b = (r * 16.0).astype(F8).astype(F32)
    r = r - b * 0.0625
    c = (r * 256.0).astype(F8).astype(F32)
    return a, b, c


def _attn_kernel(bound_ref, q_ref, kt_ref, v_ref, lq1_ref, lk1_ref, lq2_ref, lk2_ref, sg_ref,
                 glu_ref, gprev_ref, gnext_ref, cw_ref,
                 o_ref, cv_ref, vext_ref, k8_ref, kaug_ref, col_ref, acc_ref, win_ref,
                 *, lam_init, seq):
    tq = Q_BLOCK
    trips = seq // K_CHUNK
    conv_units_per_trip = tq // (CONV_ROW_CHUNK * 2 * trips)
    assert conv_units_per_trip * CONV_ROW_CHUNK * 2 * trips == tq
    assert HALO_ROWS - CONV_PAD == 1
    blk = pl.program_id(2)
    win_ref[0:HALO_ROWS, :] = jnp.where(blk == 0, 0.0, gprev_ref[0])
    win_ref[HALO_ROWS:HALO_ROWS + tq, :] = glu_ref[0]
    win_ref[HALO_ROWS + tq:, :] = jnp.where(blk == pl.num_programs(2) - 1, 0.0, gnext_ref[0])

    def conv_unit(unit):
        r0 = pl.multiple_of(unit * CONV_ROW_CHUNK, CONV_ROW_CHUNK)
        acc = None
        for b in range(SUBLANES):
            part = None
            for a in range(-(-(CONV_WIDTH + 1) // SUBLANES)):
                t = SUBLANES * a + b - 1
                if 0 <= t < CONV_WIDTH:
                    rows = pl.ds(r0 + SUBLANES * a, CONV_ROW_CHUNK + SUBLANES)
                    term = cw_ref[t:t + 1, :] * win_ref[rows, :]
                    part = term if part is None else part + term
            shifted = part[b:b + CONV_ROW_CHUNK]
            acc = shifted if acc is None else acc + shifted
        cv_ref[0, pl.ds(r0, CONV_ROW_CHUNK), :] = acc

    lane = jax.lax.broadcasted_iota(jnp.int32, (1, LANES), 1)
    first_head = lane < D_V
    comp_lanes = lane < D_COMP
    use_row_max = bound_ref[0] > SAFE_SCORE_BOUND * LOG2_E
    use_bound = jnp.logical_not(use_row_max)

    @pl.when(pl.program_id(2) == 0)
    def _prepare_keys_values():
        v = v_ref[0]
        one = jnp.ones_like(v)
        vext_ref[0] = jnp.where(first_head, v, one)
        vext_ref[1] = jnp.where(first_head, one, v)

        @pl.when(use_row_max)
        def _bf16_keys():
            row = jax.lax.broadcasted_iota(jnp.int32, (K_AUG - D_COMP, seq), 0)
            extra = jnp.where(row == 0, 1.0, 0.0).astype(BF16)
            for c in range(4):
                kaug_ref[c, 0:D_COMP, :] = kt_ref[0, c * D_COMP:(c + 1) * D_COMP, :]
                kaug_ref[c, D_COMP:, :] = extra

        @pl.when(use_bound)
        def _fp8_keys():
            for c in range(4):
                a, b, c3 = _split3(kt_ref[0, c * D_COMP:(c + 1) * D_COMP, :].astype(F32))
                blocks = (a, a * 0.25, b * 0.25, b * 0.0625, a * 0.0625, c3 * 0.0625)
                for j, blk in enumerate(blocks):
                    k8_ref[c, j * D_COMP:(j + 1) * D_COMP, :] = blk.astype(F8)
                k8_ref[c, len(blocks) * D_COMP:, :] = jnp.zeros((K_F8 - len(blocks) * D_COMP, seq), F8)

    lam = (jnp.exp(jnp.sum(lq1_ref[...] * lk1_ref[...], axis=1, keepdims=True))
           - jnp.exp(jnp.sum(lq2_ref[...] * lk2_ref[...], axis=1, keepdims=True))
           + lam_init)

    def component(c):
        qf = q_ref[0].astype(F32)
        qc = qf if c == 0 else pltpu.roll(qf, LANES - c * D_COMP, 1)
        return jnp.where(comp_lanes, qc, 0.0)

    def fp8_query_maker():
        a, b, c3 = _split3(q_ref[0].astype(F32))
        a4, a16, b4, b16, c16 = a * 0.25, a * 0.0625, b * 0.25, b * 0.0625, c3 * 0.0625
        block = lane // D_COMP

        def query_fp8(c):
            def at(x, j):
                shift = ((j - c) % (LANES // D_COMP)) * D_COMP
                return x if shift == 0 else pltpu.roll(x, shift, 1)
            lo = jnp.where(block == 0, at(a, 0),
                           jnp.where(block == 1, at(b4, 1),
                                     jnp.where(block == 2, at(a4, 2), at(b16, 3))))
            hi = jnp.where(block == 0, at(c16, 0), jnp.where(block == 1, at(a16, 1), 0.0))
            return jnp.concatenate([lo, hi], axis=1).astype(F8)
        return query_fp8

    def query_bf16(c):
        qc = component(c)
        q0 = qc[:, :K_AUG].astype(BF16)

        def mx(j, m):
            start = pl.multiple_of(j * K_SUB, K_SUB)
            s = jnp.dot(q0, kaug_ref[c, :, pl.ds(start, K_SUB)], preferred_element_type=F32)
            return jnp.maximum(m, jnp.max(s, axis=1, keepdims=True))
        col_ref[...] = -jax.lax.fori_loop(0, seq // K_SUB, mx, jnp.full((tq, 1), -jnp.inf, F32))
        qa = jnp.where(lane == D_COMP, col_ref[...], qc)
        return qa[:, :K_AUG].astype(BF16)

    def softmax_pv(head, qa1, qa2, k_ref, exp2_fn):
        acc_ref[...] = jnp.zeros_like(acc_ref)

        def body(j, carry):
            pv = None
            tiles = K_CHUNK // K_SUB
            for u in range(tiles):
                start = pl.multiple_of(j * K_CHUNK + u * K_SUB, K_SUB)
                s1 = jnp.dot(qa1, k_ref[2 * head, :, pl.ds(start, K_SUB)],
                             preferred_element_type=F32)
                s2 = jnp.dot(qa2, k_ref[2 * head + 1, :, pl.ds(start, K_SUB)],
                             preferred_element_type=F32)
                p = jnp.concatenate([exp2_fn(s1), exp2_fn(s2)], axis=0).astype(BF16)
                d = jnp.dot(p, vext_ref[head, pl.ds(start, K_SUB), :], preferred_element_type=F32)
                pv = d if pv is None else pv + d
                for w in range(u * conv_units_per_trip // tiles, (u + 1) * conv_units_per_trip // tiles):
                    conv_unit((head * trips + j) * conv_units_per_trip + w)
            acc_ref[...] += pv
            return carry

        jax.lax.fori_loop(0, trips, body, 0)
        return acc_ref[...]

    def attend(make_query, k_ref, exp2_fn):
        outs = []
        for head in range(2):
            valid = first_head if head == 0 else jnp.logical_not(first_head)
            acc = softmax_pv(head, make_query(2 * head), make_query(2 * head + 1), k_ref, exp2_fn)
            a = acc / pltpu.roll(acc, D_V, 1)
            o = a[:tq] - lam * a[tq:]
            ss = jnp.sum(jnp.where(valid, o * o, 0.0), axis=1, keepdims=True) * (1.0 / D_V)
            outs.append(o * jax.lax.rsqrt(ss + EPS))
        y = jnp.where(first_head, outs[0], outs[1]) * sg_ref[...] * (1.0 - lam_init)
        o_ref[0] = y.astype(BF16)

    @pl.when(use_bound)
    def _fp8_scores():
        attend(fp8_query_maker(), k8_ref, lambda s: jnp.exp2(s.astype(BF16)))

    @pl.when(use_row_max)
    def _bf16_scores():
        attend(query_bf16, kaug_ref, jnp.exp2)


def _attn_call(score_bound, q3, kt, v3, lq1, lk1, lq2, lk2, sg, glu3, cw, lam_init):
    batch, seq, _ = q3.shape
    pairs = N_HEADS // 2
    vec = _const_spec((1, D_COMP))
    halo_per_block = Q_BLOCK // HALO_ROWS
    n_halo_blocks = seq // HALO_ROWS
    row_block = pl.BlockSpec((1, Q_BLOCK, LANES), lambda b, h, i: (b, i, h))
    return pl.pallas_call(
        functools.partial(_attn_kernel, lam_init=lam_init, seq=seq),
        grid=(batch, pairs, seq // Q_BLOCK),
        in_specs=[
            pl.BlockSpec(memory_space=pltpu.SMEM),
            row_block,
            pl.BlockSpec((1, LANES, seq), lambda b, h, i: (b, h, 0)),
            pl.BlockSpec((1, seq, LANES), lambda b, h, i: (b, 0, h)),
            vec, vec, vec, vec,
            _const_spec((1, LANES)),
            row_block,
            pl.BlockSpec((1, HALO_ROWS, LANES),
                         lambda b, h, i: (b, jnp.maximum(i * halo_per_block - 1, 0), h)),
            pl.BlockSpec((1, HALO_ROWS, LANES),
                         lambda b, h, i: (b, jnp.minimum((i + 1) * halo_per_block, n_halo_blocks - 1), h)),
            pl.BlockSpec((CONV_WIDTH + 1, LANES), lambda b, h, i: (0, h)),
        ],
        out_specs=[row_block, row_block],
        out_shape=[jax.ShapeDtypeStruct((batch, seq, WIDTH), BF16),
                   jax.ShapeDtypeStruct((batch, seq, WIDTH), F32)],
        scratch_shapes=[
            pltpu.VMEM((2, seq, LANES), BF16),
            pltpu.VMEM((4, K_F8, seq), F8),
            pltpu.VMEM((4, K_AUG, seq), BF16),
            pltpu.VMEM((Q_BLOCK, 1), F32),
            pltpu.VMEM((2 * Q_BLOCK, LANES), F32),
            pltpu.VMEM((Q_BLOCK + 2 * HALO_ROWS, LANES), F32),
        ],
        compiler_params=pltpu.CompilerParams(
            dimension_semantics=("parallel", "parallel", "arbitrary"),
            vmem_limit_bytes=VMEM_LIMIT_BYTES),
        name="attn",
    )(score_bound, q3, kt, v3, lq1, lk1, lq2, lk2, sg, glu3, glu3, glu3, cw)


def _seqdft_kernel(zr_ref, zi_ref, m1_ref, f2_ref, o_ref, t_ref):
    def stage1(n2, carry):
        rows = pl.ds(n2, DFT_N1, stride=DFT_N2)
        z = jnp.concatenate([zr_ref[0, rows, :], zi_ref[0, rows, :]], axis=0).astype(BF16)
        t = jnp.dot(m1_ref[n2], z, preferred_element_type=F32)
        t_ref[pl.ds(pl.multiple_of(n2 * 2 * DFT_N1, 2 * DFT_N1), 2 * DFT_N1), :] = t
        return carry

    jax.lax.fori_loop(0, DFT_N2, stage1, 0, unroll=4)
    f2 = f2_ref[...]

    def stage2(k1, carry):
        tr = t_ref[pl.ds(k1, DFT_N2, stride=2 * DFT_N1), :]
        ti = t_ref[pl.ds(k1 + DFT_N1, DFT_N2, stride=2 * DFT_N1), :]
        t = jnp.concatenate([tr, ti], axis=0).astype(BF16)
        y = jnp.dot(f2, t, preferred_element_type=F32)
        o_ref[0, pl.ds(k1, DFT_N2, stride=DFT_N1), :] = y
        return carry

    jax.lax.fori_loop(0, DFT_N1, stage2, 0, unroll=8)


def _seqdft_call(zr3, zi3, m1, f2):
    batch, seq, width = zr3.shape
    z_spec = pl.BlockSpec((1, seq, LANES), lambda b, c: (b, 0, c))
    return pl.pallas_call(
        _seqdft_kernel,
        grid=(batch, width // LANES),
        in_specs=[z_spec, z_spec,
                  _const_spec((DFT_N2, 2 * DFT_N1, 2 * DFT_N1)),
                  _const_spec((DFT_N2, 2 * DFT_N2))],
        out_specs=pl.BlockSpec((1, seq, LANES), lambda b, c: (b, 0, c)),
        out_shape=jax.ShapeDtypeStruct((batch, seq, width), F32),
        scratch_shapes=[pltpu.VMEM((DFT_N2 * 2 * DFT_N1, LANES), F32)],
        compiler_params=pltpu.CompilerParams(
            dimension_semantics=("parallel", "parallel"), vmem_limit_bytes=VMEM_LIMIT_BYTES),
        name="seqdft",
    )(zr3, zi3, m1, f2)


def _dft_tables(seq, group):
    assert seq == DFT_N1 * DFT_N2
    c = np.arange(group)
    ang = 2.0 * np.pi * ((c[:, None] * c[None, :]) % group) / group
    fc = np.concatenate([np.cos(ang), np.sin(ang)], axis=1)

    k1 = np.arange(DFT_N1)[None, :, None]
    n1 = np.arange(DFT_N1)[None, None, :]
    n2 = np.arange(DFT_N2)[:, None, None]
    th = 2.0 * np.pi * ((k1 * (DFT_N2 * n1 + n2)) % seq) / seq
    co, si = np.cos(th), np.sin(th)
    m1 = np.concatenate([np.concatenate([co, si], axis=2),
                         np.concatenate([-si, co], axis=2)], axis=1)

    k2 = np.arange(DFT_N2)
    ph = 2.0 * np.pi * ((k2[:, None] * k2[None, :]) % DFT_N2) / DFT_N2
    f2 = np.concatenate([np.cos(ph), np.sin(ph)], axis=1) / math.sqrt(seq * group)
    return tuple(jnp.asarray(t, dtype=F32).astype(BF16) for t in (fc, m1, f2))


def _merge_kernel(x_ref, att_ref, cv_ref, fr_ref, gates_ref,
                  cb_ref, lng_ref, lnb_ref, wpa_ref, wpb_ref, wpc_ref, wout_ref,
                  n2g_ref, wfi_ref, wfo_ref, o_ref):
    tm = ROW_BLOCK
    cv = cv_ref[...] + cb_ref[...]
    mu = jnp.mean(cv, axis=-1, keepdims=True)
    d = cv - mu
    var = jnp.mean(d * d, axis=-1, keepdims=True)
    cv = d * jax.lax.rsqrt(var + EPS) * lng_ref[...] + lnb_ref[...]
    sb = (cv * jax.nn.sigmoid(cv)).astype(BF16)

    y_a = jnp.dot(att_ref[...], wpa_ref[...], preferred_element_type=F32)
    y_b = jnp.dot(sb, wpb_ref[...], preferred_element_type=F32)
    y_c = jnp.dot(fr_ref[...].astype(BF16), wpc_ref[...], preferred_element_type=F32)
    merged = (gates_ref[:, 0:D_MODEL].astype(F32) * y_a
              + gates_ref[:, D_MODEL:2 * D_MODEL].astype(F32) * y_b
              + gates_ref[:, 2 * D_MODEL:].astype(F32) * y_c)
    x1 = x_ref[...] + jnp.dot(merged.astype(BF16), wout_ref[...], preferred_element_type=F32)

    h2 = x1 * jax.lax.rsqrt(jnp.mean(x1 * x1, axis=-1, keepdims=True) + EPS) * n2g_ref[...]
    h2b = h2.astype(BF16)
    acc = jnp.zeros((tm, D_MODEL), F32)
    for j in range(N_FF_CHUNKS):
        gu = jnp.dot(h2b, wfi_ref[j], preferred_element_type=F32)
        gt = gu[:, :FF_CHUNK]
        a = (gt * jax.nn.sigmoid(gt) * gu[:, FF_CHUNK:]).astype(BF16)
        acc = acc + jnp.dot(a, wfo_ref[j], preferred_element_type=F32)
    o_ref[...] = x1 + acc


def _merge_call(x2, att, cv, fr, gates, cb, lng, lnb, wpa, wpb, wpc, wout, n2g, wfi, wfo):
    rows = x2.shape[0]
    tm = ROW_BLOCK
    row_spec = lambda w: pl.BlockSpec((tm, w), lambda i: (i, 0))
    return pl.pallas_call(
        _merge_kernel,
        grid=(rows // tm,),
        in_specs=[
            row_spec(D_MODEL), row_spec(WIDTH), row_spec(WIDTH),
            row_spec(WIDTH), row_spec(3 * D_MODEL),
            _const_spec((1, WIDTH)), _const_spec((1, WIDTH)), _const_spec((1, WIDTH)),
            _const_spec((WIDTH, D_MODEL)), _const_spec((WIDTH, D_MODEL)), _const_spec((WIDTH, D_MODEL)),
            _const_spec((D_MODEL, D_MODEL)),
            _const_spec((1, D_MODEL)),
            _const_spec((N_FF_CHUNKS, D_MODEL, 2 * FF_CHUNK)),
            _const_spec((N_FF_CHUNKS, FF_CHUNK, D_MODEL)),
        ],
        out_specs=row_spec(D_MODEL),
        out_shape=jax.ShapeDtypeStruct((rows, D_MODEL), F32),
        compiler_params=pltpu.CompilerParams(
            dimension_semantics=("parallel",), vmem_limit_bytes=VMEM_LIMIT_BYTES),
        name="merge_ffn",
    )(x2, att, cv, fr, gates, cb, lng, lnb, wpa, wpb, wpc, wout, n2g, wfi, wfo)


def _rope_lane_tables(seq):
    pos = jnp.arange(seq, dtype=F32)
    inv = 1.0 / (ROPE_THETA ** (jnp.arange(0, ROPE_DIMS, 2, dtype=F32) / ROPE_DIMS))
    ang = pos[:, None] * inv[None, :]
    cos, sin = jnp.cos(ang), jnp.sin(ang)
    d = np.arange(LANES) % D_COMP
    f = d % ROPE_HALF
    rc = jnp.where(d[None, :] < ROPE_DIMS, cos[:, f], 1.0)
    rs_next = jnp.where(d[None, :] < ROPE_HALF, -sin[:, f], 0.0)
    rs_prev = jnp.where((d[None, :] >= ROPE_HALF) & (d[None, :] < ROPE_DIMS), sin[:, f], 0.0)
    return rc.astype(F32), rs_next.astype(F32), rs_prev.astype(F32)


def kernel(x, norm1_g, w_in, qnorm_g, knorm_g, lambda_q1, lambda_k1, lambda_q2, lambda_k2, subln_g,
           w_proj_a, conv_w, conv_b, conv_ln_g, conv_ln_b, w_proj_b, w_proj_c, w_gate, b_gate, w_out,
           norm2_g, w_ffn_in, w_ffn_out):
    batch, seq, d = x.shape
    depth = w_in.shape[0]
    rows = batch * seq
    rc, rsn, rsp = _rope_lane_tables(seq)
    fc, m1, f2 = _dft_tables(seq, GROUP_C)
    gidx = np.arange(MXU_DIM) // D_COMP
    gmean = jnp.asarray((gidx[:, None] == gidx[None, :]) / D_COMP, dtype=F32).astype(BF16)

    x2 = x.reshape(rows, d)
    for l in range(depth):
        lam_init = 0.8 - 0.6 * math.exp(-0.3 * l)
        row = lambda a: a[l].reshape(1, -1)
        q, kt, v, glu, zr, zi, gates = _proj_call(
            x2, row(norm1_g), w_in[l].astype(BF16), w_gate[l].astype(BF16), row(b_gate),
            jnp.tile(qnorm_g[l], WIDTH // D_COMP).reshape(1, WIDTH),
            jnp.tile(knorm_g[l], WIDTH // D_COMP).reshape(1, WIDTH),
            rc, rsn, rsp, gmean, fc, batch, seq)
        score_bound = (SCORE_BOUND_MARGIN * LOG2_E * math.sqrt(D_COMP) * jnp.max(jnp.abs(qnorm_g[l]))
                       * jnp.max(jnp.abs(knorm_g[l]))).reshape(1).astype(F32)
        cw = jnp.concatenate([conv_w[l], jnp.zeros((1, WIDTH), F32)], axis=0)
        att, cv = _attn_call(
            score_bound, q.reshape(batch, seq, WIDTH), kt, v.reshape(batch, seq, WIDTH),
            row(lambda_q1), row(lambda_k1), row(lambda_q2), row(lambda_k2),
            jnp.tile(subln_g[l], LANES // D_V).reshape(1, LANES),
            glu.reshape(batch, seq, WIDTH), cw, lam_init)
        fr = _seqdft_call(zr.reshape(batch, seq, WIDTH), zi.reshape(batch, seq, WIDTH), m1, f2)
        wfi = w_ffn_in[l].astype(BF16)
        wfi = jnp.concatenate([wfi[:, :D_FF].reshape(d, N_FF_CHUNKS, FF_CHUNK),
                               wfi[:, D_FF:].reshape(d, N_FF_CHUNKS, FF_CHUNK)], axis=2)
        wfi = wfi.transpose(1, 0, 2)
        wfo = w_ffn_out[l].astype(BF16).reshape(N_FF_CHUNKS, FF_CHUNK, d)
        x2 = _merge_call(
            x2, att.reshape(rows, WIDTH), cv.reshape(rows, WIDTH), fr.reshape(rows, WIDTH), gates,
            row(conv_b), row(conv_ln_g), row(conv_ln_b),
            w_proj_a[l].astype(BF16), w_proj_b[l].astype(BF16), w_proj_c[l].astype(BF16),
            w_out[l].astype(BF16), row(norm2_g), wfi, wfo)
    return x2.reshape(batch, seq, d)
```

```python
import functools
import math

import numpy as np
import jax
import jax.numpy as jnp
from jax.experimental import pallas as pl
from jax.experimental.pallas import tpu as pltpu

D_MODEL = 1024
N_HEADS = 8
D_COMP = 32
D_V = 64
WIDTH = 512
ROPE_DIMS = 8
ROPE_HALF = 4
ROPE_THETA = 500000.0
CONV_WIDTH = 31
CONV_PAD = 15
GROUP_C = 128
D_FF = 2816
FF_CHUNK = 256
N_FF_CHUNKS = D_FF // FF_CHUNK
EPS = 1e-6

LANES = 128
SUBLANES = 8
MXU_DIM = 256
VMEM_LIMIT_BYTES = 56 * 1024 * 1024

DFT_N1 = 128
DFT_N2 = 64

ROW_BLOCK = 512
HALO_ROWS = 16
CONV_ROW_CHUNK = 64
Q_BLOCK = 1024
K_SUB = 256
K_CHUNK = 2048
K_AUG = 64
K_F8 = 256
SAFE_SCORE_BOUND = 40.0
SCORE_BOUND_MARGIN = 1.02
LOG2_E = math.log2(math.e)

BF16 = jnp.bfloat16
F8 = jnp.float8_e4m3fn
F32 = jnp.float32


def _const_spec(shape):
    nd = len(shape)
    return pl.BlockSpec(shape, lambda *_: (0,) * nd, pipeline_mode=pl.Buffered(1))


def _split_dot(a_f32, b_bf16):
    hi = a_f32.astype(BF16)
    lo = (a_f32 - hi.astype(F32)).astype(BF16)
    return (jnp.dot(hi, b_bf16, preferred_element_type=F32)
            + jnp.dot(lo, b_bf16, preferred_element_type=F32))


def _norm_rope(t, gain, group_mean, rc, rs_next, rs_prev):
    sq = t * t
    ms = jnp.concatenate([_split_dot(sq[:, j:j + MXU_DIM], group_mean)
                          for j in range(0, WIDTH, MXU_DIM)], axis=1)
    tn = t * jax.lax.rsqrt(ms + EPS) * gain
    outs = []
    for j in range(WIDTH // LANES):
        c = tn[:, j * LANES:(j + 1) * LANES]
        outs.append(c * rc + pltpu.roll(c, LANES - ROPE_HALF, 1) * rs_next
                    + pltpu.roll(c, ROPE_HALF, 1) * rs_prev)
    return jnp.concatenate(outs, axis=1)


def _proj_kernel(x_ref, n1g_ref, win_ref, wgate_ref, bgate_ref, qg_ref, kg_ref,
                 rc_ref, rsn_ref, rsp_ref, gmean_ref, fc_ref,
                 q_ref, kt_ref, v_ref, glu_ref, zr_ref, zi_ref, gates_ref):
    x = x_ref[...]
    h = x * jax.lax.rsqrt(jnp.mean(x * x, axis=-1, keepdims=True) + EPS) * n1g_ref[...]
    hb = h.astype(BF16)

    def seg(lo, width):
        return jnp.dot(hb, win_ref[:, lo:lo + width], preferred_element_type=F32)

    rc, rsn, rsp = rc_ref[...], rsn_ref[...], rsp_ref[...]
    gmean = gmean_ref[...]
    q = _norm_rope(seg(0, WIDTH), qg_ref[...], gmean, rc, rsn, rsp)
    q_ref[...] = (q * (D_COMP ** -0.5 * LOG2_E)).astype(BF16)
    k = _norm_rope(seg(WIDTH, WIDTH), kg_ref[...], gmean, rc, rsn, rsp)
    kt_ref[0] = k.T.astype(BF16)
    v_ref[...] = seg(2 * WIDTH, WIDTH).astype(BF16)

    ga = seg(3 * WIDTH, WIDTH)
    gb = seg(4 * WIDTH, WIDTH)
    glu_ref[...] = ga * jax.nn.sigmoid(gb)

    f = seg(5 * WIDTH, WIDTH)
    fc = fc_ref[...]
    for j in range(WIDTH // GROUP_C):
        ab = jnp.dot(f[:, j * GROUP_C:(j + 1) * GROUP_C].astype(BF16), fc,
                     preferred_element_type=F32)
        zr_ref[:, j * GROUP_C:(j + 1) * GROUP_C] = ab[:, :GROUP_C]
        zi_ref[:, j * GROUP_C:(j + 1) * GROUP_C] = -ab[:, GROUP_C:]

    for j in range(3):
        lo = j * D_MODEL
        gt = jnp.dot(hb, wgate_ref[:, lo:lo + D_MODEL], preferred_element_type=F32)
        gates_ref[:, lo:lo + D_MODEL] = jax.nn.sigmoid(gt + bgate_ref[:, lo:lo + D_MODEL]).astype(BF16)


def _proj_call(x2, n1g, win, wgate, bgate, qg, kg, rc, rsn, rsp, gmean, fc, batch, seq):
    rows = x2.shape[0]
    tm = ROW_BLOCK
    blocks_per_seq = seq // tm
    row_spec = lambda w: pl.BlockSpec((tm, w), lambda i: (i, 0))
    tab_spec = pl.BlockSpec((tm, LANES), lambda i: (i % blocks_per_seq, 0))
    in_width = win.shape[1]
    return pl.pallas_call(
        _proj_kernel,
        grid=(rows // tm,),
        in_specs=[
            row_spec(D_MODEL),
            _const_spec((1, D_MODEL)),
            _const_spec((D_MODEL, in_width)),
            _const_spec((D_MODEL, 3 * D_MODEL)),
            _const_spec((1, 3 * D_MODEL)),
            _const_spec((1, WIDTH)),
            _const_spec((1, WIDTH)),
            tab_spec, tab_spec, tab_spec,
            _const_spec((MXU_DIM, MXU_DIM)),
            _const_spec((GROUP_C, 2 * GROUP_C)),
        ],
        out_specs=[
            row_spec(WIDTH),
            pl.BlockSpec((1, WIDTH, tm), lambda i: (i // blocks_per_seq, 0, i % blocks_per_seq)),
            row_spec(WIDTH),
            row_spec(WIDTH),
            row_spec(WIDTH),
            row_spec(WIDTH),
            row_spec(3 * D_MODEL),
        ],
        out_shape=[
            jax.ShapeDtypeStruct((rows, WIDTH), BF16),
            jax.ShapeDtypeStruct((batch, WIDTH, seq), BF16),
            jax.ShapeDtypeStruct((rows, WIDTH), BF16),
            jax.ShapeDtypeStruct((rows, WIDTH), F32),
            jax.ShapeDtypeStruct((rows, WIDTH), F32),
            jax.ShapeDtypeStruct((rows, WIDTH), F32),
            jax.ShapeDtypeStruct((rows, 3 * D_MODEL), BF16),
        ],
        compiler_params=pltpu.CompilerParams(
            dimension_semantics=("parallel",), vmem_limit_bytes=VMEM_LIMIT_BYTES),
        name="proj",
    )(x2, n1g, win, wgate, bgate, qg, kg, rc, rsn, rsp, gmean, fc)


def _split3(x):
    a = x.astype(F8).astype(F32)
    r = x - a
    b = (r * 16.0).astype(F8).astype(F32)
    r = r - b * 0.0625
    c = (r * 256.0).astype(F8).astype(F32)
    return a, b, c


def _attn_kernel(bound_ref, q_ref, kt_ref, v_ref, lq1_ref, lk1_ref, lq2_ref, lk2_ref, sg_ref,
                 glu_ref, gprev_ref, gnext_ref, cw_ref,
                 o_ref, cv_ref, vext_ref, k8_ref, kaug_ref, col_ref, acc_ref, win_ref,
                 *, lam_init, seq):
    tq = Q_BLOCK
    trips = seq // K_CHUNK
    conv_units_per_trip = tq // (CONV_ROW_CHUNK * 2 * trips)
    assert conv_units_per_trip * CONV_ROW_CHUNK * 2 * trips == tq
    assert HALO_ROWS - CONV_PAD == 1
    blk = pl.program_id(2)
    win_ref[0:HALO_ROWS, :] = jnp.where(blk == 0, 0.0, gprev_ref[0])
    win_ref[HALO_ROWS:HALO_ROWS + tq, :] = glu_ref[0]
    win_ref[HALO_ROWS + tq:, :] = jnp.where(blk == pl.num_programs(2) - 1, 0.0, gnext_ref[0])

    def conv_unit(unit):
        r0 = pl.multiple_of(unit * CONV_ROW_CHUNK, CONV_ROW_CHUNK)
        acc = None
        for b in range(SUBLANES):
            part = None
            for a in range(-(-(CONV_WIDTH + 1) // SUBLANES)):
                t = SUBLANES * a + b - 1
                if 0 <= t < CONV_WIDTH:
                    rows = pl.ds(r0 + SUBLANES * a, CONV_ROW_CHUNK + SUBLANES)
                    term = cw_ref[t:t + 1, :] * win_ref[rows, :]
                    part = term if part is None else part + term
            shifted = part[b:b + CONV_ROW_CHUNK]
            acc = shifted if acc is None else acc + shifted
        cv_ref[0, pl.ds(r0, CONV_ROW_CHUNK), :] = acc

    lane = jax.lax.broadcasted_iota(jnp.int32, (1, LANES), 1)
    first_head = lane < D_V
    comp_lanes = lane < D_COMP
    use_row_max = bound_ref[0] > SAFE_SCORE_BOUND * LOG2_E
    use_bound = jnp.logical_not(use_row_max)

    @pl.when(pl.program_id(2) == 0)
    def _prepare_keys_values():
        v = v_ref[0]
        one = jnp.ones_like(v)
        vext_ref[0] = jnp.where(first_head, v, one)
        vext_ref[1] = jnp.where(first_head, one, v)

        @pl.when(use_row_max)
        def _bf16_keys():
            row = jax.lax.broadcasted_iota(jnp.int32, (K_AUG - D_COMP, seq), 0)
            extra = jnp.where(row == 0, 1.0, 0.0).astype(BF16)
            for c in range(4):
                kaug_ref[c, 0:D_COMP, :] = kt_ref[0, c * D_COMP:(c + 1) * D_COMP, :]
                kaug_ref[c, D_COMP:, :] = extra

        @pl.when(use_bound)
        def _fp8_keys():
            for c in range(4):
                a, b, c3 = _split3(kt_ref[0, c * D_COMP:(c + 1) * D_COMP, :].astype(F32))
                blocks = (a, a * 0.25, b * 0.25, b * 0.0625, a * 0.0625, c3 * 0.0625)
                for j, blk in enumerate(blocks):
                    k8_ref[c, j * D_COMP:(j + 1) * D_COMP, :] = blk.astype(F8)
                k8_ref[c, len(blocks) * D_COMP:, :] = jnp.zeros((K_F8 - len(blocks) * D_COMP, seq), F8)

    lam = (jnp.exp(jnp.sum(lq1_ref[...] * lk1_ref[...], axis=1, keepdims=True))
           - jnp.exp(jnp.sum(lq2_ref[...] * lk2_ref[...], axis=1, keepdims=True))
           + lam_init)

    def component(c):
        qf = q_ref[0].astype(F32)
        qc = qf if c == 0 else pltpu.roll(qf, LANES - c * D_COMP, 1)
        return jnp.where(comp_lanes, qc, 0.0)

    def fp8_query_maker():
        a, b, c3 = _split3(q_ref[0].astype(F32))
        a4, a16, b4, b16, c16 = a * 0.25, a * 0.0625, b * 0.25, b * 0.0625, c3 * 0.0625
        block = lane // D_COMP

        def query_fp8(c):
            def at(x, j):
                shift = ((j - c) % (LANES // D_COMP)) * D_COMP
                return x if shift == 0 else pltpu.roll(x, shift, 1)
            lo = jnp.where(block == 0, at(a, 0),
                           jnp.where(block == 1, at(b4, 1),
                                     jnp.where(block == 2, at(a4, 2), at(b16, 3))))
            hi = jnp.where(block == 0, at(c16, 0), jnp.where(block == 1, at(a16, 1), 0.0))
            return jnp.concatenate([lo, hi], axis=1).astype(F8)
        return query_fp8

    def query_bf16(c):
        qc = component(c)
        q0 = qc[:, :K_AUG].astype(BF16)

        def mx(j, m):
            start = pl.multiple_of(j * K_SUB, K_SUB)
            s = jnp.dot(q0, kaug_ref[c, :, pl.ds(start, K_SUB)], preferred_element_type=F32)
            return jnp.maximum(m, jnp.max(s, axis=1, keepdims=True))
        col_ref[...] = -jax.lax.fori_loop(0, seq // K_SUB, mx, jnp.full((tq, 1), -jnp.inf, F32))
        qa = jnp.where(lane == D_COMP, col_ref[...], qc)
        return qa[:, :K_AUG].astype(BF16)

    def softmax_pv(head, qa1, qa2, k_ref, exp2_fn):
        acc_ref[...] = jnp.zeros_like(acc_ref)

        def body(j, carry):
            pv = None
            tiles = K_CHUNK // K_SUB
            for u in range(tiles):
                start = pl.multiple_of(j * K_CHUNK + u * K_SUB, K_SUB)
                s1 = jnp.dot(qa1, k_ref[2 * head, :, pl.ds(start, K_SUB)],
                             preferred_element_type=F32)
                s2 = jnp.dot(qa2, k_ref[2 * head + 1, :, pl.ds(start, K_SUB)],
                             preferred_element_type=F32)
                p = jnp.concatenate([exp2_fn(s1), exp2_fn(s2)], axis=0).astype(BF16)
                d = jnp.dot(p, vext_ref[head, pl.ds(start, K_SUB), :], preferred_element_type=F32)
                pv = d if pv is None else pv + d
                for w in range(u * conv_units_per_trip // tiles, (u + 1) * conv_units_per_trip // tiles):
                    conv_unit((head * trips + j) * conv_units_per_trip + w)
            acc_ref[...] += pv
            return carry

        jax.lax.fori_loop(0, trips, body, 0)
        return acc_ref[...]

    def attend(make_query, k_ref, exp2_fn):
        outs = []
        for head in range(2):
            valid = first_head if head == 0 else jnp.logical_not(first_head)
            acc = softmax_pv(head, make_query(2 * head), make_query(2 * head + 1), k_ref, exp2_fn)
            a = acc / pltpu.roll(acc, D_V, 1)
            o = a[:tq] - lam * a[tq:]
            ss = jnp.sum(jnp.where(valid, o * o, 0.0), axis=1, keepdims=True) * (1.0 / D_V)
            outs.append(o * jax.lax.rsqrt(ss + EPS))
        y = jnp.where(first_head, outs[0], outs[1]) * sg_ref[...] * (1.0 - lam_init)
        o_ref[0] = y.astype(BF16)

    @pl.when(use_bound)
    def _fp8_scores():
        attend(fp8_query_maker(), k8_ref, lambda s: jnp.exp2(s.astype(BF16)))

    @pl.when(use_row_max)
    def _bf16_scores():
        attend(query_bf16, kaug_ref, jnp.exp2)


def _attn_call(score_bound, q3, kt, v3, lq1, lk1, lq2, lk2, sg, glu3, cw, lam_init):
    batch, seq, _ = q3.shape
    pairs = N_HEADS // 2
    vec = _const_spec((1, D_COMP))
    halo_per_block = Q_BLOCK // HALO_ROWS
    n_halo_blocks = seq // HALO_ROWS
    row_block = pl.BlockSpec((1, Q_BLOCK, LANES), lambda b, h, i: (b, i, h))
    return pl.pallas_call(
        functools.partial(_attn_kernel, lam_init=lam_init, seq=seq),
        grid=(batch, pairs, seq // Q_BLOCK),
        in_specs=[
            pl.BlockSpec(memory_space=pltpu.SMEM),
            row_block,
            pl.BlockSpec((1, LANES, seq), lambda b, h, i: (b, h, 0)),
            pl.BlockSpec((1, seq, LANES), lambda b, h, i: (b, 0, h)),
            vec, vec, vec, vec,
            _const_spec((1, LANES)),
            row_block,
            pl.BlockSpec((1, HALO_ROWS, LANES),
                         lambda b, h, i: (b, jnp.maximum(i * halo_per_block - 1, 0), h)),
            pl.BlockSpec((1, HALO_ROWS, LANES),
                         lambda b, h, i: (b, jnp.minimum((i + 1) * halo_per_block, n_halo_blocks - 1), h)),
            pl.BlockSpec((CONV_WIDTH + 1, LANES), lambda b, h, i: (0, h)),
        ],
        out_specs=[row_block, row_block],
        out_shape=[jax.ShapeDtypeStruct((batch, seq, WIDTH), BF16),
                   jax.ShapeDtypeStruct((batch, seq, WIDTH), F32)],
        scratch_shapes=[
            pltpu.VMEM((2, seq, LANES), BF16),
            pltpu.VMEM((4, K_F8, seq), F8),
            pltpu.VMEM((4, K_AUG, seq), BF16),
            pltpu.VMEM((Q_BLOCK, 1), F32),
            pltpu.VMEM((2 * Q_BLOCK, LANES), F32),
            pltpu.VMEM((Q_BLOCK + 2 * HALO_ROWS, LANES), F32),
        ],
        compiler_params=pltpu.CompilerParams(
            dimension_semantics=("parallel", "parallel", "arbitrary"),
            vmem_limit_bytes=VMEM_LIMIT_BYTES),
        name="attn",
    )(score_bound, q3, kt, v3, lq1, lk1, lq2, lk2, sg, glu3, glu3, glu3, cw)


def _seqdft_kernel(zr_ref, zi_ref, m1_ref, f2_ref, o_ref, t_ref):
    def stage1(n2, carry):
        rows = pl.ds(n2, DFT_N1, stride=DFT_N2)
        z = jnp.concatenate([zr_ref[0, rows, :], zi_ref[0, rows, :]], axis=0).astype(BF16)
        t = jnp.dot(m1_ref[n2], z, preferred_element_type=F32)
        t_ref[pl.ds(pl.multiple_of(n2 * 2 * DFT_N1, 2 * DFT_N1), 2 * DFT_N1), :] = t
        return carry

    jax.lax.fori_loop(0, DFT_N2, stage1, 0, unroll=4)
    f2 = f2_ref[...]

    def stage2(k1, carry):
        tr = t_ref[pl.ds(k1, DFT_N2, stride=2 * DFT_N1), :]
        ti = t_ref[pl.ds(k1 + DFT_N1, DFT_N2, stride=2 * DFT_N1), :]
        t = jnp.concatenate([tr, ti], axis=0).astype(BF16)
        y = jnp.dot(f2, t, preferred_element_type=F32)
        o_ref[0, pl.ds(k1, DFT_N2, stride=DFT_N1), :] = y
        return carry

    jax.lax.fori_loop(0, DFT_N1, stage2, 0, unroll=8)


def _seqdft_call(zr3, zi3, m1, f2):
    batch, seq, width = zr3.shape
    z_spec = pl.BlockSpec((1, seq, LANES), lambda b, c: (b, 0, c))
    return pl.pallas_call(
        _seqdft_kernel,
        grid=(batch, width // LANES),
        in_specs=[z_spec, z_spec,
                  _const_spec((DFT_N2, 2 * DFT_N1, 2 * DFT_N1)),
                  _const_spec((DFT_N2, 2 * DFT_N2))],
        out_specs=pl.BlockSpec((1, seq, LANES), lambda b, c: (b, 0, c)),
        out_shape=jax.ShapeDtypeStruct((batch, seq, width), F32),
        scratch_shapes=[pltpu.VMEM((DFT_N2 * 2 * DFT_N1, LANES), F32)],
        compiler_params=pltpu.CompilerParams(
            dimension_semantics=("parallel", "parallel"), vmem_limit_bytes=VMEM_LIMIT_BYTES),
        name="seqdft",
    )(zr3, zi3, m1, f2)


def _dft_tables(seq, group):
    assert seq == DFT_N1 * DFT_N2
    c = np.arange(group)
    ang = 2.0 * np.pi * ((c[:, None] * c[None, :]) % group) / group
    fc = np.concatenate([np.cos(ang), np.sin(ang)], axis=1)

    k1 = np.arange(DFT_N1)[None, :, None]
    n1 = np.arange(DFT_N1)[None, None, :]
    n2 = np.arange(DFT_N2)[:, None, None]
    th = 2.0 * np.pi * ((k1 * (DFT_N2 * n1 + n2)) % seq) / seq
    co, si = np.cos(th), np.sin(th)
    m1 = np.concatenate([np.concatenate([co, si], axis=2),
                         np.concatenate([-si, co], axis=2)], axis=1)

    k2 = np.arange(DFT_N2)
    ph = 2.0 * np.pi * ((k2[:, None] * k2[None, :]) % DFT_N2) / DFT_N2
    f2 = np.concatenate([np.cos(ph), np.sin(ph)], axis=1) / math.sqrt(seq * group)
    return tuple(jnp.asarray(t, dtype=F32).astype(BF16) for t in (fc, m1, f2))


def _merge_kernel(x_ref, att_ref, cv_ref, fr_ref, gates_ref,
                  cb_ref, lng_ref, lnb_ref, wpa_ref, wpb_ref, wpc_ref, wout_ref,
                  n2g_ref, wfi_ref, wfo_ref, o_ref):
    tm = ROW_BLOCK
    cv = cv_ref[...] + cb_ref[...]
    mu = jnp.mean(cv, axis=-1, keepdims=True)
    d = cv - mu
    var = jnp.mean(d * d, axis=-1, keepdims=True)
    cv = d * jax.lax.rsqrt(var + EPS) * lng_ref[...] + lnb_ref[...]
    sb = (cv * jax.nn.sigmoid(cv)).astype(BF16)

    y_a = jnp.dot(att_ref[...], wpa_ref[...], preferred_element_type=F32)
    y_b = jnp.dot(sb, wpb_ref[...], preferred_element_type=F32)
    y_c = jnp.dot(fr_ref[...].astype(BF16), wpc_ref[...], preferred_element_type=F32)
    merged = (gates_ref[:, 0:D_MODEL].astype(F32) * y_a
              + gates_ref[:, D_MODEL:2 * D_MODEL].astype(F32) * y_b
              + gates_ref[:, 2 * D_MODEL:].astype(F32) * y_c)
    x1 = x_ref[...] + jnp.dot(merged.astype(BF16), wout_ref[...], preferred_element_type=F32)

    h2 = x1 * jax.lax.rsqrt(jnp.mean(x1 * x1, axis=-1, keepdims=True) + EPS) * n2g_ref[...]
    h2b = h2.astype(BF16)
    acc = jnp.zeros((tm, D_MODEL), F32)
    for j in range(N_FF_CHUNKS):
        gu = jnp.dot(h2b, wfi_ref[j], preferred_element_type=F32)
        gt = gu[:, :FF_CHUNK]
        a = (gt * jax.nn.sigmoid(gt) * gu[:, FF_CHUNK:]).astype(BF16)
        acc = acc + jnp.dot(a, wfo_ref[j], preferred_element_type=F32)
    o_ref[...] = x1 + acc


def _merge_call(x2, att, cv, fr, gates, cb, lng, lnb, wpa, wpb, wpc, wout, n2g, wfi, wfo):
    rows = x2.shape[0]
    tm = ROW_BLOCK
    row_spec = lambda w: pl.BlockSpec((tm, w), lambda i: (i, 0))
    return pl.pallas_call(
        _merge_kernel,
        grid=(rows // tm,),
        in_specs=[
            row_spec(D_MODEL), row_spec(WIDTH), row_spec(WIDTH),
            row_spec(WIDTH), row_spec(3 * D_MODEL),
            _const_spec((1, WIDTH)), _const_spec((1, WIDTH)), _const_spec((1, WIDTH)),
            _const_spec((WIDTH, D_MODEL)), _const_spec((WIDTH, D_MODEL)), _const_spec((WIDTH, D_MODEL)),
            _const_spec((D_MODEL, D_MODEL)),
            _const_spec((1, D_MODEL)),
            _const_spec((N_FF_CHUNKS, D_MODEL, 2 * FF_CHUNK)),
            _const_spec((N_FF_CHUNKS, FF_CHUNK, D_MODEL)),
        ],
        out_specs=row_spec(D_MODEL),
        out_shape=jax.ShapeDtypeStruct((rows, D_MODEL), F32),
        compiler_params=pltpu.CompilerParams(
            dimension_semantics=("parallel",), vmem_limit_bytes=VMEM_LIMIT_BYTES),
        name="merge_ffn",
    )(x2, att, cv, fr, gates, cb, lng, lnb, wpa, wpb, wpc, wout, n2g, wfi, wfo)


def _rope_lane_tables(seq):
    pos = jnp.arange(seq, dtype=F32)
    inv = 1.0 / (ROPE_THETA ** (jnp.arange(0, ROPE_DIMS, 2, dtype=F32) / ROPE_DIMS))
    ang = pos[:, None] * inv[None, :]
    cos, sin = jnp.cos(ang), jnp.sin(ang)
    d = np.arange(LANES) % D_COMP
    f = d % ROPE_HALF
    rc = jnp.where(d[None, :] < ROPE_DIMS, cos[:, f], 1.0)
    rs_next = jnp.where(d[None, :] < ROPE_HALF, -sin[:, f], 0.0)
    rs_prev = jnp.where((d[None, :] >= ROPE_HALF) & (d[None, :] < ROPE_DIMS), sin[:, f], 0.0)
    return rc.astype(F32), rs_next.astype(F32), rs_prev.astype(F32)


def kernel(x, norm1_g, w_in, qnorm_g, knorm_g, lambda_q1, lambda_k1, lambda_q2, lambda_k2, subln_g,
           w_proj_a, conv_w, conv_b, conv_ln_g, conv_ln_b, w_proj_b, w_proj_c, w_gate, b_gate, w_out,
           norm2_g, w_ffn_in, w_ffn_out):
    batch, seq, d = x.shape
    depth = w_in.shape[0]
    rows = batch * seq
    rc, rsn, rsp = _rope_lane_tables(seq)
    fc, m1, f2 = _dft_tables(seq, GROUP_C)
    gidx = np.arange(MXU_DIM) // D_COMP
    gmean = jnp.asarray((gidx[:, None] == gidx[None, :]) / D_COMP, dtype=F32).astype(BF16)

    x2 = x.reshape(rows, d)
    for l in range(depth):
        lam_init = 0.8 - 0.6 * math.exp(-0.3 * l)
        row = lambda a: a[l].reshape(1, -1)
        q, kt, v, glu, zr, zi, gates = _proj_call(
            x2, row(norm1_g), w_in[l].astype(BF16), w_gate[l].astype(BF16), row(b_gate),
            jnp.tile(qnorm_g[l], WIDTH // D_COMP).reshape(1, WIDTH),
            jnp.tile(knorm_g[l], WIDTH // D_COMP).reshape(1, WIDTH),
            rc, rsn, rsp, gmean, fc, batch, seq)
        score_bound = (SCORE_BOUND_MARGIN * LOG2_E * math.sqrt(D_COMP) * jnp.max(jnp.abs(qnorm_g[l]))
                       * jnp.max(jnp.abs(knorm_g[l]))).reshape(1).astype(F32)
        cw = jnp.concatenate([conv_w[l], jnp.zeros((1, WIDTH), F32)], axis=0)
        att, cv = _attn_call(
            score_bound, q.reshape(batch, seq, WIDTH), kt, v.reshape(batch, seq, WIDTH),
            row(lambda_q1), row(lambda_k1), row(lambda_q2), row(lambda_k2),
            jnp.tile(subln_g[l], LANES // D_V).reshape(1, LANES),
            glu.reshape(batch, seq, WIDTH), cw, lam_init)
        fr = _seqdft_call(zr.reshape(batch, seq, WIDTH), zi.reshape(batch, seq, WIDTH), m1, f2)
        wfi = w_ffn_in[l].astype(BF16)
        wfi = jnp.concatenate([wfi[:, :D_FF].reshape(d, N_FF_CHUNKS, FF_CHUNK),
                               wfi[:, D_FF:].reshape(d, N_FF_CHUNKS, FF_CHUNK)], axis=2)
        wfi = wfi.transpose(1, 0, 2)
        wfo = w_ffn_out[l].astype(BF16).reshape(N_FF_CHUNKS, FF_CHUNK, d)
        x2 = _merge_call(
            x2, att.reshape(rows, WIDTH), cv.reshape(rows, WIDTH), fr.reshape(rows, WIDTH), gates,
            row(conv_b), row(conv_ln_g), row(conv_ln_b),
            w_proj_a[l].astype(BF16), w_proj_b[l].astype(BF16), w_proj_c[l].astype(BF16),
            w_out[l].astype(BF16), row(norm2_g), wfi, wfo)
    return x2.reshape(batch, seq, d)
```

```python
import functools
import math

import numpy as np
import jax
import jax.numpy as jnp
from jax.experimental import pallas as pl
from jax.experimental.pallas import tpu as pltpu

D_MODEL = 1024
N_HEADS = 8
D_COMP = 32
D_V = 64
WIDTH = 512
ROPE_DIMS = 8
ROPE_HALF = 4
ROPE_THETA = 500000.0
CONV_WIDTH = 31
CONV_PAD = 15
GROUP_C = 128
D_FF = 2816
FF_CHUNK = 256
N_FF_CHUNKS = D_FF // FF_CHUNK
EPS = 1e-6

LANES = 128
SUBLANES = 8
MXU_DIM = 256
VMEM_LIMIT_BYTES = 56 * 1024 * 1024

DFT_N1 = 128
DFT_N2 = 64

ROW_BLOCK = 512
HALO_ROWS = 16
CONV_ROW_CHUNK = 64
Q_BLOCK = 1024
K_SUB = 256
K_CHUNK = 2048
K_AUG = 64
K_F8 = 256
SAFE_SCORE_BOUND = 40.0
SCORE_BOUND_MARGIN = 1.02
LOG2_E = math.log2(math.e)

BF16 = jnp.bfloat16
F8 = jnp.float8_e4m3fn
F32 = jnp.float32


def _const_spec(shape):
    nd = len(shape)
    return pl.BlockSpec(shape, lambda *_: (0,) * nd, pipeline_mode=pl.Buffered(1))


def _split_dot(a_f32, b_bf16):
    hi = a_f32.astype(BF16)
    lo = (a_f32 - hi.astype(F32)).astype(BF16)
    return (jnp.dot(hi, b_bf16, preferred_element_type=F32)
            + jnp.dot(lo, b_bf16, preferred_element_type=F32))


def _norm_rope(t, gain, group_mean, rc, rs_next, rs_prev):
    sq = t * t
    ms = jnp.concatenate([_split_dot(sq[:, j:j + MXU_DIM], group_mean)
                          for j in range(0, WIDTH, MXU_DIM)], axis=1)
    tn = t * jax.lax.rsqrt(ms + EPS) * gain
    outs = []
    for j in range(WIDTH // LANES):
        c = tn[:, j * LANES:(j + 1) * LANES]
        outs.append(c * rc + pltpu.roll(c, LANES - ROPE_HALF, 1) * rs_next
                    + pltpu.roll(c, ROPE_HALF, 1) * rs_prev)
    return jnp.concatenate(outs, axis=1)


def _proj_kernel(x_ref, n1g_ref, win_ref, wgate_ref, bgate_ref, qg_ref, kg_ref,
                 rc_ref, rsn_ref, rsp_ref, gmean_ref, fc_ref,
                 q_ref, kt_ref, v_ref, glu_ref, zr_ref, zi_ref, gates_ref):
    x = x_ref[...]
    h = x * jax.lax.rsqrt(jnp.mean(x * x, axis=-1, keepdims=True) + EPS) * n1g_ref[...]
    hb = h.astype(BF16)

    def seg(lo, width):
        return jnp.dot(hb, win_ref[:, lo:lo + width], preferred_element_type=F32)

    rc, rsn, rsp = rc_ref[...], rsn_ref[...], rsp_ref[...]
    gmean = gmean_ref[...]
    q = _norm_rope(seg(0, WIDTH), qg_ref[...], gmean, rc, rsn, rsp)
    q_ref[...] = (q * (D_COMP ** -0.5 * LOG2_E)).astype(BF16)
    k = _norm_rope(seg(WIDTH, WIDTH), kg_ref[...], gmean, rc, rsn, rsp)
    kt_ref[0] = k.T.astype(BF16)
    v_ref[...] = seg(2 * WIDTH, WIDTH).astype(BF16)

    ga = seg(3 * WIDTH, WIDTH)
    gb = seg(4 * WIDTH, WIDTH)
    glu_ref[...] = ga * jax.nn.sigmoid(gb)

    f = seg(5 * WIDTH, WIDTH)
    fc = fc_ref[...]
    for j in range(WIDTH // GROUP_C):
        ab = jnp.dot(f[:, j * GROUP_C:(j + 1) * GROUP_C].astype(BF16), fc,
                     preferred_element_type=F32)
        zr_ref[:, j * GROUP_C:(j + 1) * GROUP_C] = ab[:, :GROUP_C]
        zi_ref[:, j * GROUP_C:(j + 1) * GROUP_C] = -ab[:, GROUP_C:]

    for j in range(3):
        lo = j * D_MODEL
        gt = jnp.dot(hb, wgate_ref[:, lo:lo + D_MODEL], preferred_element_type=F32)
        gates_ref[:, lo:lo + D_MODEL] = jax.nn.sigmoid(gt + bgate_ref[:, lo:lo + D_MODEL]).astype(BF16)


def _proj_call(x2, n1g, win, wgate, bgate, qg, kg, rc, rsn, rsp, gmean, fc, batch, seq):
    rows = x2.shape[0]
    tm = ROW_BLOCK
    blocks_per_seq = seq // tm
    row_spec = lambda w: pl.BlockSpec((tm, w), lambda i: (i, 0))
    tab_spec = pl.BlockSpec((tm, LANES), lambda i: (i % blocks_per_seq, 0))
    in_width = win.shape[1]
    return pl.pallas_call(
        _proj_kernel,
        grid=(rows // tm,),
        in_specs=[
            row_spec(D_MODEL),
            _const_spec((1, D_MODEL)),
            _const_spec((D_MODEL, in_width)),
            _const_spec((D_MODEL, 3 * D_MODEL)),
            _const_spec((1, 3 * D_MODEL)),
            _const_spec((1, WIDTH)),
            _const_spec((1, WIDTH)),
            tab_spec, tab_spec, tab_spec,
            _const_spec((MXU_DIM, MXU_DIM)),
            _const_spec((GROUP_C, 2 * GROUP_C)),
        ],
        out_specs=[
            row_spec(WIDTH),
            pl.BlockSpec((1, WIDTH, tm), lambda i: (i // blocks_per_seq, 0, i % blocks_per_seq)),
            row_spec(WIDTH),
            row_spec(WIDTH),
            row_spec(WIDTH),
            row_spec(WIDTH),
            row_spec(3 * D_MODEL),
        ],
        out_shape=[
            jax.ShapeDtypeStruct((rows, WIDTH), BF16),
            jax.ShapeDtypeStruct((batch, WIDTH, seq), BF16),
            jax.ShapeDtypeStruct((rows, WIDTH), BF16),
            jax.ShapeDtypeStruct((rows, WIDTH), F32),
            jax.ShapeDtypeStruct((rows, WIDTH), F32),
            jax.ShapeDtypeStruct((rows, WIDTH), F32),
            jax.ShapeDtypeStruct((rows, 3 * D_MODEL), BF16),
        ],
        compiler_params=pltpu.CompilerParams(
            dimension_semantics=("parallel",), vmem_limit_bytes=VMEM_LIMIT_BYTES),
        name="proj",
    )(x2, n1g, win, wgate, bgate, qg, kg, rc, rsn, rsp, gmean, fc)


def _split3(x):
    a = x.astype(F8).astype(F32)
    r = x - a
    b = (r * 16.0).astype(F8).astype(F32)
    r = r - b * 0.0625
    c = (r * 256.0).astype(F8).astype(F32)
    return a, b, c


def _attn_kernel(bound_ref, q_ref, kt_ref, v_ref, lq1_ref, lk1_ref, lq2_ref, lk2_ref, sg_ref,
                 glu_ref, gprev_ref, gnext_ref, cw_ref,
                 o_ref, cv_ref, vext_ref, k8_ref, kaug_ref, col_ref, acc_ref, win_ref,
                 *, lam_init, seq):
    tq = Q_BLOCK
    trips = seq // K_CHUNK
    conv_units_per_trip = tq // (CONV_ROW_CHUNK * 2 * trips)
    assert conv_units_per_trip * CONV_ROW_CHUNK * 2 * trips == tq
    assert HALO_ROWS - CONV_PAD == 1
    blk = pl.program_id(2)
    win_ref[0:HALO_ROWS, :] = jnp.where(blk == 0, 0.0, gprev_ref[0])
    win_ref[HALO_ROWS:HALO_ROWS + tq, :] = glu_ref[0]
    win_ref[HALO_ROWS + tq:, :] = jnp.where(blk == pl.num_programs(2) - 1, 0.0, gnext_ref[0])

    def conv_unit(unit):
        r0 = pl.multiple_of(unit * CONV_ROW_CHUNK, CONV_ROW_CHUNK)
        acc = None
        for b in range(SUBLANES):
            part = None
            for a in range(-(-(CONV_WIDTH + 1) // SUBLANES)):
                t = SUBLANES * a + b - 1
                if 0 <= t < CONV_WIDTH:
                    rows = pl.ds(r0 + SUBLANES * a, CONV_ROW_CHUNK + SUBLANES)
                    term = cw_ref[t:t + 1, :] * win_ref[rows, :]
                    part = term if part is None else part + term
            shifted = part[b:b + CONV_ROW_CHUNK]
            acc = shifted if acc is None else acc + shifted
        cv_ref[0, pl.ds(r0, CONV_ROW_CHUNK), :] = acc

    lane = jax.lax.broadcasted_iota(jnp.int32, (1, LANES), 1)
    first_head = lane < D_V
    comp_lanes = lane < D_COMP
    use_row_max = bound_ref[0] > SAFE_SCORE_BOUND * LOG2_E
    use_bound = jnp.logical_not(use_row_max)

    @pl.when(pl.program_id(2) == 0)
    def _prepare_keys_values():
        v = v_ref[0]
        one = jnp.ones_like(v)
        vext_ref[0] = jnp.where(first_head, v, one)
        vext_ref[1] = jnp.where(first_head, one, v)

        @pl.when(use_row_max)
        def _bf16_keys():
            row = jax.lax.broadcasted_iota(jnp.int32, (K_AUG - D_COMP, seq), 0)
            extra = jnp.where(row == 0, 1.0, 0.0).astype(BF16)
            for c in range(4):
                kaug_ref[c, 0:D_COMP, :] = kt_ref[0, c * D_COMP:(c + 1) * D_COMP, :]
                kaug_ref[c, D_COMP:, :] = extra

        @pl.when(use_bound)
        def _fp8_keys():
            nb = LANES // D_COMP
            zero = jnp.zeros((D_COMP, seq), F8)
            for c in range(4):
                a, b, c3 = _split3(kt_ref[0, c * D_COMP:(c + 1) * D_COMP, :].astype(F32))
                lo = [t.astype(F8) for t in (a, a * 0.25, b * 0.25, b * 0.0625)]
                hi = [(a * 0.0625).astype(F8), (c3 * 0.0625).astype(F8), zero, zero]
                for p in range(nb):
                    k8_ref[c, p * D_COMP:(p + 1) * D_COMP, :] = lo[(p - c) % nb]
                    k8_ref[c, (nb + p) * D_COMP:(nb + p + 1) * D_COMP, :] = hi[(p - c) % nb]

    lam = (jnp.exp(jnp.sum(lq1_ref[...] * lk1_ref[...], axis=1, keepdims=True))
           - jnp.exp(jnp.sum(lq2_ref[...] * lk2_ref[...], axis=1, keepdims=True))
           + lam_init)

    def component(c):
        qf = q_ref[0].astype(F32)
        qc = qf if c == 0 else pltpu.roll(qf, LANES - c * D_COMP, 1)
        return jnp.where(comp_lanes, qc, 0.0)

    def fp8_query_maker():
        a, b, c3 = _split3(q_ref[0].astype(F32))
        nb = LANES // D_COMP
        block = lane // D_COMP
        lo_terms = [a] + [pltpu.roll(t, j * D_COMP, 1)
                          for j, t in ((1, b * 0.25), (2, a * 0.25), (3, b * 0.0625))]
        hi_terms = [c3 * 0.0625, pltpu.roll(a * 0.0625, D_COMP, 1)]

        def query_fp8(c):
            lo = lo_terms[nb - 1]
            for j in range(nb - 2, -1, -1):
                lo = jnp.where(block == (c + j) % nb, lo_terms[j], lo)
            hi = jnp.where(block == c, hi_terms[0],
                           jnp.where(block == (c + 1) % nb, hi_terms[1], 0.0))
            return jnp.concatenate([lo, hi], axis=1).astype(F8)
        return query_fp8

    def query_bf16(c):
        qc = component(c)
        q0 = qc[:, :K_AUG].astype(BF16)

        def mx(j, m):
            start = pl.multiple_of(j * K_SUB, K_SUB)
            s = jnp.dot(q0, kaug_ref[c, :, pl.ds(start, K_SUB)], preferred_element_type=F32)
            return jnp.maximum(m, jnp.max(s, axis=1, keepdims=True))
        col_ref[...] = -jax.lax.fori_loop(0, seq // K_SUB, mx, jnp.full((tq, 1), -jnp.inf, F32))
        qa = jnp.where(lane == D_COMP, col_ref[...], qc)
        return qa[:, :K_AUG].astype(BF16)

    def softmax_pv(head, qa1, qa2, k_ref, exp2_fn):
        acc_ref[...] = jnp.zeros_like(acc_ref)

        def body(j, carry):
            pv = None
            tiles = K_CHUNK // K_SUB
            for u in range(tiles):
                start = pl.multiple_of(j * K_CHUNK + u * K_SUB, K_SUB)
                s1 = jnp.dot(qa1, k_ref[2 * head, :, pl.ds(start, K_SUB)],
                             preferred_element_type=F32)
                s2 = jnp.dot(qa2, k_ref[2 * head + 1, :, pl.ds(start, K_SUB)],
                             preferred_element_type=F32)
                p = jnp.concatenate([exp2_fn(s1), exp2_fn(s2)], axis=0).astype(BF16)
                d = jnp.dot(p, vext_ref[head, pl.ds(start, K_SUB), :], preferred_element_type=F32)
                pv = d if pv is None else pv + d
                for w in range(u * conv_units_per_trip // tiles, (u + 1) * conv_units_per_trip // tiles):
                    conv_unit((head * trips + j) * conv_units_per_trip + w)
            acc_ref[...] += pv
            return carry

        jax.lax.fori_loop(0, trips, body, 0)
        return acc_ref[...]

    def attend(make_query, k_ref, exp2_fn):
        outs = []
        for head in range(2):
            valid = first_head if head == 0 else jnp.logical_not(first_head)
            acc = softmax_pv(head, make_query(2 * head), make_query(2 * head + 1), k_ref, exp2_fn)
            a = acc / pltpu.roll(acc, D_V, 1)
            o = a[:tq] - lam * a[tq:]
            ss = jnp.sum(jnp.where(valid, o * o, 0.0), axis=1, keepdims=True) * (1.0 / D_V)
            outs.append(o * jax.lax.rsqrt(ss + EPS))
        y = jnp.where(first_head, outs[0], outs[1]) * sg_ref[...] * (1.0 - lam_init)
        o_ref[0] = y.astype(BF16)

    @pl.when(use_bound)
    def _fp8_scores():
        attend(fp8_query_maker(), k8_ref, lambda s: jnp.exp2(s.astype(BF16)))

    @pl.when(use_row_max)
    def _bf16_scores():
        attend(query_bf16, kaug_ref, jnp.exp2)


def _attn_call(score_bound, q3, kt, v3, lq1, lk1, lq2, lk2, sg, glu3, cw, lam_init):
    batch, seq, _ = q3.shape
    pairs = N_HEADS // 2
    vec = _const_spec((1, D_COMP))
    halo_per_block = Q_BLOCK // HALO_ROWS
    n_halo_blocks = seq // HALO_ROWS
    row_block = pl.BlockSpec((1, Q_BLOCK, LANES), lambda b, h, i: (b, i, h))
    return pl.pallas_call(
        functools.partial(_attn_kernel, lam_init=lam_init, seq=seq),
        grid=(batch, pairs, seq // Q_BLOCK),
        in_specs=[
            pl.BlockSpec(memory_space=pltpu.SMEM),
            row_block,
            pl.BlockSpec((1, LANES, seq), lambda b, h, i: (b, h, 0)),
            pl.BlockSpec((1, seq, LANES), lambda b, h, i: (b, 0, h)),
            vec, vec, vec, vec,
            _const_spec((1, LANES)),
            row_block,
            pl.BlockSpec((1, HALO_ROWS, LANES),
                         lambda b, h, i: (b, jnp.maximum(i * halo_per_block - 1, 0), h)),
            pl.BlockSpec((1, HALO_ROWS, LANES),
                         lambda b, h, i: (b, jnp.minimum((i + 1) * halo_per_block, n_halo_blocks - 1), h)),
            pl.BlockSpec((CONV_WIDTH + 1, LANES), lambda b, h, i: (0, h)),
        ],
        out_specs=[row_block, row_block],
        out_shape=[jax.ShapeDtypeStruct((batch, seq, WIDTH), BF16),
                   jax.ShapeDtypeStruct((batch, seq, WIDTH), F32)],
        scratch_shapes=[
            pltpu.VMEM((2, seq, LANES), BF16),
            pltpu.VMEM((4, K_F8, seq), F8),
            pltpu.VMEM((4, K_AUG, seq), BF16),
            pltpu.VMEM((Q_BLOCK, 1), F32),
            pltpu.VMEM((2 * Q_BLOCK, LANES), F32),
            pltpu.VMEM((Q_BLOCK + 2 * HALO_ROWS, LANES), F32),
        ],
        compiler_params=pltpu.CompilerParams(
            dimension_semantics=("parallel", "parallel", "arbitrary"),
            vmem_limit_bytes=VMEM_LIMIT_BYTES),
        name="attn",
    )(score_bound, q3, kt, v3, lq1, lk1, lq2, lk2, sg, glu3, glu3, glu3, cw)


def _seqdft_kernel(zr_ref, zi_ref, m1_ref, f2_ref, o_ref, t_ref):
    def stage1(n2, carry):
        rows = pl.ds(n2, DFT_N1, stride=DFT_N2)
        z = jnp.concatenate([zr_ref[0, rows, :], zi_ref[0, rows, :]], axis=0).astype(BF16)
        t = jnp.dot(m1_ref[n2], z, preferred_element_type=F32)
        t_ref[pl.ds(pl.multiple_of(n2 * 2 * DFT_N1, 2 * DFT_N1), 2 * DFT_N1), :] = t
        return carry

    jax.lax.fori_loop(0, DFT_N2, stage1, 0, unroll=4)
    f2 = f2_ref[...]

    def stage2(k1, carry):
        tr = t_ref[pl.ds(k1, DFT_N2, stride=2 * DFT_N1), :]
        ti = t_ref[pl.ds(k1 + DFT_N1, DFT_N2, stride=2 * DFT_N1), :]
        t = jnp.concatenate([tr, ti], axis=0).astype(BF16)
        y = jnp.dot(f2, t, preferred_element_type=F32)
        o_ref[0, pl.ds(k1, DFT_N2, stride=DFT_N1), :] = y
        return carry

    jax.lax.fori_loop(0, DFT_N1, stage2, 0, unroll=8)


def _seqdft_call(zr3, zi3, m1, f2):
    batch, seq, width = zr3.shape
    z_spec = pl.BlockSpec((1, seq, LANES), lambda b, c: (b, 0, c))
    return pl.pallas_call(
        _seqdft_kernel,
        grid=(batch, width // LANES),
        in_specs=[z_spec, z_spec,
                  _const_spec((DFT_N2, 2 * DFT_N1, 2 * DFT_N1)),
                  _const_spec((DFT_N2, 2 * DFT_N2))],
        out_specs=pl.BlockSpec((1, seq, LANES), lambda b, c: (b, 0, c)),
        out_shape=jax.ShapeDtypeStruct((batch, seq, width), F32),
        scratch_shapes=[pltpu.VMEM((DFT_N2 * 2 * DFT_N1, LANES), F32)],
        compiler_params=pltpu.CompilerParams(
            dimension_semantics=("parallel", "parallel"), vmem_limit_bytes=VMEM_LIMIT_BYTES),
        name="seqdft",
    )(zr3, zi3, m1, f2)


def _dft_tables(seq, group):
    assert seq == DFT_N1 * DFT_N2
    c = np.arange(group)
    ang = 2.0 * np.pi * ((c[:, None] * c[None, :]) % group) / group
    fc = np.concatenate([np.cos(ang), np.sin(ang)], axis=1)

    k1 = np.arange(DFT_N1)[None, :, None]
    n1 = np.arange(DFT_N1)[None, None, :]
    n2 = np.arange(DFT_N2)[:, None, None]
    th = 2.0 * np.pi * ((k1 * (DFT_N2 * n1 + n2)) % seq) / seq
    co, si = np.cos(th), np.sin(th)
    m1 = np.concatenate([np.concatenate([co, si], axis=2),
                         np.concatenate([-si, co], axis=2)], axis=1)

    k2 = np.arange(DFT_N2)
    ph = 2.0 * np.pi * ((k2[:, None] * k2[None, :]) % DFT_N2) / DFT_N2
    f2 = np.concatenate([np.cos(ph), np.sin(ph)], axis=1) / math.sqrt(seq * group)
    return tuple(jnp.asarray(t, dtype=F32).astype(BF16) for t in (fc, m1, f2))


def _merge_kernel(x_ref, att_ref, cv_ref, fr_ref, gates_ref,
                  cb_ref, lng_ref, lnb_ref, wpa_ref, wpb_ref, wpc_ref, wout_ref,
                  n2g_ref, wfi_ref, wfo_ref, o_ref):
    tm = ROW_BLOCK
    cv = cv_ref[...] + cb_ref[...]
    mu = jnp.mean(cv, axis=-1, keepdims=True)
    d = cv - mu
    var = jnp.mean(d * d, axis=-1, keepdims=True)
    cv = d * jax.lax.rsqrt(var + EPS) * lng_ref[...] + lnb_ref[...]
    sb = (cv * jax.nn.sigmoid(cv)).astype(BF16)

    y_a = jnp.dot(att_ref[...], wpa_ref[...], preferred_element_type=F32)
    y_b = jnp.dot(sb, wpb_ref[...], preferred_element_type=F32)
    y_c = jnp.dot(fr_ref[...].astype(BF16), wpc_ref[...], preferred_element_type=F32)
    merged = (gates_ref[:, 0:D_MODEL].astype(F32) * y_a
              + gates_ref[:, D_MODEL:2 * D_MODEL].astype(F32) * y_b
              + gates_ref[:, 2 * D_MODEL:].astype(F32) * y_c)
    x1 = x_ref[...] + jnp.dot(merged.astype(BF16), wout_ref[...], preferred_element_type=F32)

    h2 = x1 * jax.lax.rsqrt(jnp.mean(x1 * x1, axis=-1, keepdims=True) + EPS) * n2g_ref[...]
    h2b = h2.astype(BF16)
    acc = jnp.zeros((tm, D_MODEL), F32)
    for j in range(N_FF_CHUNKS):
        lo = j * FF_CHUNK
        gt = jnp.dot(h2b, wfi_ref[:, lo:lo + FF_CHUNK], preferred_element_type=F32)
        up = jnp.dot(h2b, wfi_ref[:, D_FF + lo:D_FF + lo + FF_CHUNK], preferred_element_type=F32)
        a = (gt * jax.nn.sigmoid(gt) * up).astype(BF16)
        acc = acc + jnp.dot(a, wfo_ref[j], preferred_element_type=F32)
    o_ref[...] = x1 + acc


def _merge_call(x2, att, cv, fr, gates, cb, lng, lnb, wpa, wpb, wpc, wout, n2g, wfi, wfo):
    rows = x2.shape[0]
    tm = ROW_BLOCK
    row_spec = lambda w: pl.BlockSpec((tm, w), lambda i: (i, 0))
    return pl.pallas_call(
        _merge_kernel,
        grid=(rows // tm,),
        in_specs=[
            row_spec(D_MODEL), row_spec(WIDTH), row_spec(WIDTH),
            row_spec(WIDTH), row_spec(3 * D_MODEL),
            _const_spec((1, WIDTH)), _const_spec((1, WIDTH)), _const_spec((1, WIDTH)),
            _const_spec((WIDTH, D_MODEL)), _const_spec((WIDTH, D_MODEL)), _const_spec((WIDTH, D_MODEL)),
            _const_spec((D_MODEL, D_MODEL)),
            _const_spec((1, D_MODEL)),
            _const_spec((D_MODEL, 2 * D_FF)),
            _const_spec((N_FF_CHUNKS, FF_CHUNK, D_MODEL)),
        ],
        out_specs=row_spec(D_MODEL),
        out_shape=jax.ShapeDtypeStruct((rows, D_MODEL), F32),
        compiler_params=pltpu.CompilerParams(
            dimension_semantics=("parallel",), vmem_limit_bytes=VMEM_LIMIT_BYTES),
        name="merge_ffn",
    )(x2, att, cv, fr, gates, cb, lng, lnb, wpa, wpb, wpc, wout, n2g, wfi, wfo)


def _rope_lane_tables(seq):
    pos = jnp.arange(seq, dtype=F32)
    inv = 1.0 / (ROPE_THETA ** (jnp.arange(0, ROPE_DIMS, 2, dtype=F32) / ROPE_DIMS))
    ang = pos[:, None] * inv[None, :]
    cos, sin = jnp.cos(ang), jnp.sin(ang)
    d = np.arange(LANES) % D_COMP
    f = d % ROPE_HALF
    rc = jnp.where(d[None, :] < ROPE_DIMS, cos[:, f], 1.0)
    rs_next = jnp.where(d[None, :] < ROPE_HALF, -sin[:, f], 0.0)
    rs_prev = jnp.where((d[None, :] >= ROPE_HALF) & (d[None, :] < ROPE_DIMS), sin[:, f], 0.0)
    return rc.astype(F32), rs_next.astype(F32), rs_prev.astype(F32)


def kernel(x, norm1_g, w_in, qnorm_g, knorm_g, lambda_q1, lambda_k1, lambda_q2, lambda_k2, subln_g,
           w_proj_a, conv_w, conv_b, conv_ln_g, conv_ln_b, w_proj_b, w_proj_c, w_gate, b_gate, w_out,
           norm2_g, w_ffn_in, w_ffn_out):
    batch, seq, d = x.shape
    depth = w_in.shape[0]
    rows = batch * seq
    rc, rsn, rsp = _rope_lane_tables(seq)
    fc, m1, f2 = _dft_tables(seq, GROUP_C)
    gidx = np.arange(MXU_DIM) // D_COMP
    gmean = jnp.asarray((gidx[:, None] == gidx[None, :]) / D_COMP, dtype=F32).astype(BF16)

    x2 = x.reshape(rows, d)
    for l in range(depth):
        lam_init = 0.8 - 0.6 * math.exp(-0.3 * l)
        row = lambda a: a[l].reshape(1, -1)
        q, kt, v, glu, zr, zi, gates = _proj_call(
            x2, row(norm1_g), w_in[l].astype(BF16), w_gate[l].astype(BF16), row(b_gate),
            jnp.tile(qnorm_g[l], WIDTH // D_COMP).reshape(1, WIDTH),
            jnp.tile(knorm_g[l], WIDTH // D_COMP).reshape(1, WIDTH),
            rc, rsn, rsp, gmean, fc, batch, seq)
        score_bound = (SCORE_BOUND_MARGIN * LOG2_E * math.sqrt(D_COMP) * jnp.max(jnp.abs(qnorm_g[l]))
                       * jnp.max(jnp.abs(knorm_g[l]))).reshape(1).astype(F32)
        cw = jnp.concatenate([conv_w[l], jnp.zeros((1, WIDTH), F32)], axis=0)
        att, cv = _attn_call(
            score_bound, q.reshape(batch, seq, WIDTH), kt, v.reshape(batch, seq, WIDTH),
            row(lambda_q1), row(lambda_k1), row(lambda_q2), row(lambda_k2),
            jnp.tile(subln_g[l], LANES // D_V).reshape(1, LANES),
            glu.reshape(batch, seq, WIDTH), cw, lam_init)
        fr = _seqdft_call(zr.reshape(batch, seq, WIDTH), zi.reshape(batch, seq, WIDTH), m1, f2)
        wfi = w_ffn_in[l].astype(BF16)
        wfo = w_ffn_out[l].astype(BF16).reshape(N_FF_CHUNKS, FF_CHUNK, d)
        x2 = _merge_call(
            x2, att.reshape(rows, WIDTH), cv.reshape(rows, WIDTH), fr.reshape(rows, WIDTH), gates,
            row(conv_b), row(conv_ln_g), row(conv_ln_b),
            w_proj_a[l].astype(BF16), w_proj_b[l].astype(BF16), w_proj_c[l].astype(BF16),
            w_out[l].astype(BF16), row(norm2_g), wfi, wfo)
    return x2.reshape(batch, seq, d)
```

```python
import functools
import math

import numpy as np
import jax
import jax.numpy as jnp
from jax.experimental import pallas as pl
from jax.experimental.pallas import tpu as pltpu

D_MODEL = 1024
N_HEADS = 8
D_COMP = 32
D_V = 64
WIDTH = 512
ROPE_DIMS = 8
ROPE_HALF = 4
ROPE_THETA = 500000.0
CONV_WIDTH = 31
CONV_PAD = 15
GROUP_C = 128
D_FF = 2816
FF_CHUNK = 256
N_FF_CHUNKS = D_FF // FF_CHUNK
EPS = 1e-6

LANES = 128
SUBLANES = 8
MXU_DIM = 256
VMEM_LIMIT_BYTES = 56 * 1024 * 1024

DFT_N1 = 128
DFT_N2 = 64

ROW_BLOCK = 512
HALO_ROWS = 16
CONV_ROW_CHUNK = 64
Q_BLOCK = 1024
K_SUB = 512
K_CHUNK = 2048
K_AUG = 64
K_F8 = 256
SAFE_SCORE_BOUND = 40.0
SCORE_BOUND_MARGIN = 1.02
LOG2_E = math.log2(math.e)

BF16 = jnp.bfloat16
F8 = jnp.float8_e4m3fn
F32 = jnp.float32


def _const_spec(shape):
    nd = len(shape)
    return pl.BlockSpec(shape, lambda *_: (0,) * nd, pipeline_mode=pl.Buffered(1))


def _split_dot(a_f32, b_bf16):
    hi = a_f32.astype(BF16)
    lo = (a_f32 - hi.astype(F32)).astype(BF16)
    return (jnp.dot(hi, b_bf16, preferred_element_type=F32)
            + jnp.dot(lo, b_bf16, preferred_element_type=F32))


def _norm_rope(t, gain, group_mean, rc, rs_next, rs_prev):
    sq = t * t
    ms = jnp.concatenate([_split_dot(sq[:, j:j + MXU_DIM], group_mean)
                          for j in range(0, WIDTH, MXU_DIM)], axis=1)
    tn = t * jax.lax.rsqrt(ms + EPS) * gain
    outs = []
    for j in range(WIDTH // LANES):
        c = tn[:, j * LANES:(j + 1) * LANES]
        outs.append(c * rc + pltpu.roll(c, LANES - ROPE_HALF, 1) * rs_next
                    + pltpu.roll(c, ROPE_HALF, 1) * rs_prev)
    return jnp.concatenate(outs, axis=1)


def _proj_kernel(x_ref, n1g_ref, win_ref, wgate_ref, bgate_ref, qg_ref, kg_ref,
                 rc_ref, rsn_ref, rsp_ref, gmean_ref, fc_ref,
                 q_ref, kt_ref, v_ref, glu_ref, zr_ref, zi_ref, gates_ref):
    x = x_ref[...]
    h = x * jax.lax.rsqrt(jnp.mean(x * x, axis=-1, keepdims=True) + EPS) * n1g_ref[...]
    hb = h.astype(BF16)

    def seg(lo, width):
        return jnp.dot(hb, win_ref[:, lo:lo + width], preferred_element_type=F32)

    rc, rsn, rsp = rc_ref[...], rsn_ref[...], rsp_ref[...]
    gmean = gmean_ref[...]
    q = _norm_rope(seg(0, WIDTH), qg_ref[...], gmean, rc, rsn, rsp)
    q_ref[...] = (q * (D_COMP ** -0.5 * LOG2_E)).astype(BF16)
    k = _norm_rope(seg(WIDTH, WIDTH), kg_ref[...], gmean, rc, rsn, rsp)
    kt_ref[0] = k.T.astype(BF16)
    v_ref[...] = seg(2 * WIDTH, WIDTH).astype(BF16)

    ga = seg(3 * WIDTH, WIDTH)
    gb = seg(4 * WIDTH, WIDTH)
    glu_ref[...] = ga * jax.nn.sigmoid(gb)

    f = seg(5 * WIDTH, WIDTH)
    fc = fc_ref[...]
    for j in range(WIDTH // GROUP_C):
        ab = jnp.dot(f[:, j * GROUP_C:(j + 1) * GROUP_C].astype(BF16), fc,
                     preferred_element_type=F32)
        zr_ref[:, j * GROUP_C:(j + 1) * GROUP_C] = ab[:, :GROUP_C]
        zi_ref[:, j * GROUP_C:(j + 1) * GROUP_C] = -ab[:, GROUP_C:]

    for j in range(3):
        lo = j * D_MODEL
        gt = jnp.dot(hb, wgate_ref[:, lo:lo + D_MODEL], preferred_element_type=F32)
        gates_ref[:, lo:lo + D_MODEL] = jax.nn.sigmoid(gt + bgate_ref[:, lo:lo + D_MODEL]).astype(BF16)


def _proj_call(x2, n1g, win, wgate, bgate, qg, kg, rc, rsn, rsp, gmean, fc, batch, seq):
    rows = x2.shape[0]
    tm = ROW_BLOCK
    blocks_per_seq = seq // tm
    row_spec = lambda w: pl.BlockSpec((tm, w), lambda i: (i, 0))
    tab_spec = pl.BlockSpec((tm, LANES), lambda i: (i % blocks_per_seq, 0))
    in_width = win.shape[1]
    return pl.pallas_call(
        _proj_kernel,
        grid=(rows // tm,),
        in_specs=[
            row_spec(D_MODEL),
            _const_spec((1, D_MODEL)),
            _const_spec((D_MODEL, in_width)),
            _const_spec((D_MODEL, 3 * D_MODEL)),
            _const_spec((1, 3 * D_MODEL)),
            _const_spec((1, WIDTH)),
            _const_spec((1, WIDTH)),
            tab_spec, tab_spec, tab_spec,
            _const_spec((MXU_DIM, MXU_DIM)),
            _const_spec((GROUP_C, 2 * GROUP_C)),
        ],
        out_specs=[
            row_spec(WIDTH),
            pl.BlockSpec((1, WIDTH, tm), lambda i: (i // blocks_per_seq, 0, i % blocks_per_seq)),
            row_spec(WIDTH),
            row_spec(WIDTH),
            row_spec(WIDTH),
            row_spec(WIDTH),
            row_spec(3 * D_MODEL),
        ],
        out_shape=[
            jax.ShapeDtypeStruct((rows, WIDTH), BF16),
            jax.ShapeDtypeStruct((batch, WIDTH, seq), BF16),
            jax.ShapeDtypeStruct((rows, WIDTH), BF16),
            jax.ShapeDtypeStruct((rows, WIDTH), F32),
            jax.ShapeDtypeStruct((rows, WIDTH), F32),
            jax.ShapeDtypeStruct((rows, WIDTH), F32),
            jax.ShapeDtypeStruct((rows, 3 * D_MODEL), BF16),
        ],
        compiler_params=pltpu.CompilerParams(
            dimension_semantics=("parallel",), vmem_limit_bytes=VMEM_LIMIT_BYTES),
        name="proj",
    )(x2, n1g, win, wgate, bgate, qg, kg, rc, rsn, rsp, gmean, fc)


def _split3(x):
    a = x.astype(F8).astype(F32)
    r = x - a
    b = (r * 16.0).astype(F8).astype(F32)
    r = r - b * 0.0625
    c = (r * 256.0).astype(F8).astype(F32)
    return a, b, c


def _attn_kernel(bound_ref, q_ref, kt_ref, v_ref, lq1_ref, lk1_ref, lq2_ref, lk2_ref, sg_ref,
                 glu_ref, gprev_ref, gnext_ref, cw_ref,
                 o_ref, cv_ref, vext_ref, k8_ref, kaug_ref, col_ref, acc_ref, win_ref,
                 *, lam_init, seq):
    tq = Q_BLOCK
    trips = seq // K_CHUNK
    conv_units_per_trip = tq // (CONV_ROW_CHUNK * 2 * trips)
    assert conv_units_per_trip * CONV_ROW_CHUNK * 2 * trips == tq
    assert HALO_ROWS - CONV_PAD == 1
    blk = pl.program_id(2)
    win_ref[0:HALO_ROWS, :] = jnp.where(blk == 0, 0.0, gprev_ref[0])
    win_ref[HALO_ROWS:HALO_ROWS + tq, :] = glu_ref[0]
    win_ref[HALO_ROWS + tq:, :] = jnp.where(blk == pl.num_programs(2) - 1, 0.0, gnext_ref[0])

    def conv_unit(unit):
        r0 = pl.multiple_of(unit * CONV_ROW_CHUNK, CONV_ROW_CHUNK)
        acc = None
        for b in range(SUBLANES):
            part = None
            for a in range(-(-(CONV_WIDTH + 1) // SUBLANES)):
                t = SUBLANES * a + b - 1
                if 0 <= t < CONV_WIDTH:
                    rows = pl.ds(r0 + SUBLANES * a, CONV_ROW_CHUNK + SUBLANES)
                    term = cw_ref[t:t + 1, :] * win_ref[rows, :]
                    part = term if part is None else part + term
            shifted = part[b:b + CONV_ROW_CHUNK]
            acc = shifted if acc is None else acc + shifted
        cv_ref[0, pl.ds(r0, CONV_ROW_CHUNK), :] = acc

    lane = jax.lax.broadcasted_iota(jnp.int32, (1, LANES), 1)
    first_head = lane < D_V
    comp_lanes = lane < D_COMP
    use_row_max = bound_ref[0] > SAFE_SCORE_BOUND * LOG2_E
    use_bound = jnp.logical_not(use_row_max)

    @pl.when(pl.program_id(2) == 0)
    def _prepare_keys_values():
        v = v_ref[0]
        one = jnp.ones_like(v)
        vext_ref[0] = jnp.where(first_head, v, one)
        vext_ref[1] = jnp.where(first_head, one, v)

        @pl.when(use_row_max)
        def _bf16_keys():
            row = jax.lax.broadcasted_iota(jnp.int32, (K_AUG - D_COMP, seq), 0)
            extra = jnp.where(row == 0, 1.0, 0.0).astype(BF16)
            for c in range(4):
                kaug_ref[c, 0:D_COMP, :] = kt_ref[0, c * D_COMP:(c + 1) * D_COMP, :]
                kaug_ref[c, D_COMP:, :] = extra

        @pl.when(use_bound)
        def _fp8_keys():
            nb = LANES // D_COMP
            zero = jnp.zeros((D_COMP, seq), F8)
            for c in range(4):
                a, b, c3 = _split3(kt_ref[0, c * D_COMP:(c + 1) * D_COMP, :].astype(F32))
                lo = [t.astype(F8) for t in (a, a * 0.25, b * 0.25, b * 0.0625)]
                hi = [(a * 0.0625).astype(F8), (c3 * 0.0625).astype(F8), zero, zero]
                for p in range(nb):
                    k8_ref[c, p * D_COMP:(p + 1) * D_COMP, :] = lo[(p - c) % nb]
                    k8_ref[c, (nb + p) * D_COMP:(nb + p + 1) * D_COMP, :] = hi[(p - c) % nb]

    lam = (jnp.exp(jnp.sum(lq1_ref[...] * lk1_ref[...], axis=1, keepdims=True))
           - jnp.exp(jnp.sum(lq2_ref[...] * lk2_ref[...], axis=1, keepdims=True))
           + lam_init)

    def component(c):
        qf = q_ref[0].astype(F32)
        qc = qf if c == 0 else pltpu.roll(qf, LANES - c * D_COMP, 1)
        return jnp.where(comp_lanes, qc, 0.0)

    def fp8_query_maker():
        a, b, c3 = _split3(q_ref[0].astype(F32))
        nb = LANES // D_COMP
        block = lane // D_COMP
        lo_terms = [a] + [pltpu.roll(t, j * D_COMP, 1)
                          for j, t in ((1, b * 0.25), (2, a * 0.25), (3, b * 0.0625))]
        hi_terms = [c3 * 0.0625, pltpu.roll(a * 0.0625, D_COMP, 1)]

        def query_fp8(c):
            lo = lo_terms[nb - 1]
            for j in range(nb - 2, -1, -1):
                lo = jnp.where(block == (c + j) % nb, lo_terms[j], lo)
            hi = jnp.where(block == c, hi_terms[0],
                           jnp.where(block == (c + 1) % nb, hi_terms[1], 0.0))
            return jnp.concatenate([lo, hi], axis=1).astype(F8)
        return query_fp8

    def query_bf16(c):
        qc = component(c)
        q0 = qc[:, :K_AUG].astype(BF16)

        def mx(j, m):
            start = pl.multiple_of(j * K_SUB, K_SUB)
            s = jnp.dot(q0, kaug_ref[c, :, pl.ds(start, K_SUB)], preferred_element_type=F32)
            return jnp.maximum(m, jnp.max(s, axis=1, keepdims=True))
        col_ref[...] = -jax.lax.fori_loop(0, seq // K_SUB, mx, jnp.full((tq, 1), -jnp.inf, F32))
        qa = jnp.where(lane == D_COMP, col_ref[...], qc)
        return qa[:, :K_AUG].astype(BF16)

    def softmax_pv(head, qa1, qa2, k_ref, exp2_fn):
        acc_ref[...] = jnp.zeros_like(acc_ref)

        def body(j, carry):
            pv = None
            tiles = K_CHUNK // K_SUB
            for u in range(tiles):
                start = pl.multiple_of(j * K_CHUNK + u * K_SUB, K_SUB)
                s1 = jnp.dot(qa1, k_ref[2 * head, :, pl.ds(start, K_SUB)],
                             preferred_element_type=F32)
                s2 = jnp.dot(qa2, k_ref[2 * head + 1, :, pl.ds(start, K_SUB)],
                             preferred_element_type=F32)
                p = jnp.concatenate([exp2_fn(s1), exp2_fn(s2)], axis=0).astype(BF16)
                d = jnp.dot(p, vext_ref[head, pl.ds(start, K_SUB), :], preferred_element_type=F32)
                pv = d if pv is None else pv + d
                for w in range(u * conv_units_per_trip // tiles, (u + 1) * conv_units_per_trip // tiles):
                    conv_unit((head * trips + j) * conv_units_per_trip + w)
            acc_ref[...] += pv
            return carry

        jax.lax.fori_loop(0, trips, body, 0)
        return acc_ref[...]

    def attend(make_query, k_ref, exp2_fn):
        outs = []
        for head in range(2):
            valid = first_head if head == 0 else jnp.logical_not(first_head)
            acc = softmax_pv(head, make_query(2 * head), make_query(2 * head + 1), k_ref, exp2_fn)
            a = acc / pltpu.roll(acc, D_V, 1)
            o = a[:tq] - lam * a[tq:]
            ss = jnp.sum(jnp.where(valid, o * o, 0.0), axis=1, keepdims=True) * (1.0 / D_V)
            outs.append(o * jax.lax.rsqrt(ss + EPS))
        y = jnp.where(first_head, outs[0], outs[1]) * sg_ref[...] * (1.0 - lam_init)
        o_ref[0] = y.astype(BF16)

    @pl.when(use_bound)
    def _fp8_scores():
        attend(fp8_query_maker(), k8_ref, lambda s: jnp.exp2(s.astype(BF16)))

    @pl.when(use_row_max)
    def _bf16_scores():
        attend(query_bf16, kaug_ref, jnp.exp2)


def _attn_call(score_bound, q3, kt, v3, lq1, lk1, lq2, lk2, sg, glu3, cw, lam_init):
    batch, seq, _ = q3.shape
    pairs = N_HEADS // 2
    vec = _const_spec((1, D_COMP))
    halo_per_block = Q_BLOCK // HALO_ROWS
    n_halo_blocks = seq // HALO_ROWS
    row_block = pl.BlockSpec((1, Q_BLOCK, LANES), lambda b, h, i: (b, i, h))
    return pl.pallas_call(
        functools.partial(_attn_kernel, lam_init=lam_init, seq=seq),
        grid=(batch, pairs, seq // Q_BLOCK),
        in_specs=[
            pl.BlockSpec(memory_space=pltpu.SMEM),
            row_block,
            pl.BlockSpec((1, LANES, seq), lambda b, h, i: (b, h, 0)),
            pl.BlockSpec((1, seq, LANES), lambda b, h, i: (b, 0, h)),
            vec, vec, vec, vec,
            _const_spec((1, LANES)),
            row_block,
            pl.BlockSpec((1, HALO_ROWS, LANES),
                         lambda b, h, i: (b, jnp.maximum(i * halo_per_block - 1, 0), h)),
            pl.BlockSpec((1, HALO_ROWS, LANES),
                         lambda b, h, i: (b, jnp.minimum((i + 1) * halo_per_block, n_halo_blocks - 1), h)),
            pl.BlockSpec((CONV_WIDTH + 1, LANES), lambda b, h, i: (0, h)),
        ],
        out_specs=[row_block, row_block],
        out_shape=[jax.ShapeDtypeStruct((batch, seq, WIDTH), BF16),
                   jax.ShapeDtypeStruct((batch, seq, WIDTH), F32)],
        scratch_shapes=[
            pltpu.VMEM((2, seq, LANES), BF16),
            pltpu.VMEM((4, K_F8, seq), F8),
            pltpu.VMEM((4, K_AUG, seq), BF16),
            pltpu.VMEM((Q_BLOCK, 1), F32),
            pltpu.VMEM((2 * Q_BLOCK, LANES), F32),
            pltpu.VMEM((Q_BLOCK + 2 * HALO_ROWS, LANES), F32),
        ],
        compiler_params=pltpu.CompilerParams(
            dimension_semantics=("parallel", "parallel", "arbitrary"),
            vmem_limit_bytes=VMEM_LIMIT_BYTES),
        name="attn",
    )(score_bound, q3, kt, v3, lq1, lk1, lq2, lk2, sg, glu3, glu3, glu3, cw)


def _seqdft_kernel(zr_ref, zi_ref, m1_ref, f2_ref, o_ref, t_ref):
    def stage1(n2, carry):
        rows = pl.ds(n2, DFT_N1, stride=DFT_N2)
        z = jnp.concatenate([zr_ref[0, rows, :], zi_ref[0, rows, :]], axis=0).astype(BF16)
        t = jnp.dot(m1_ref[n2], z, preferred_element_type=F32)
        t_ref[pl.ds(pl.multiple_of(n2 * 2 * DFT_N1, 2 * DFT_N1), 2 * DFT_N1), :] = t
        return carry

    jax.lax.fori_loop(0, DFT_N2, stage1, 0, unroll=8)
    f2 = f2_ref[...]

    def stage2(k1, carry):
        tr = t_ref[pl.ds(k1, DFT_N2, stride=2 * DFT_N1), :]
        ti = t_ref[pl.ds(k1 + DFT_N1, DFT_N2, stride=2 * DFT_N1), :]
        t = jnp.concatenate([tr, ti], axis=0).astype(BF16)
        y = jnp.dot(f2, t, preferred_element_type=F32)
        o_ref[0, pl.ds(k1, DFT_N2, stride=DFT_N1), :] = y
        return carry

    jax.lax.fori_loop(0, DFT_N1, stage2, 0, unroll=8)


def _seqdft_call(zr3, zi3, m1, f2):
    batch, seq, width = zr3.shape
    z_spec = pl.BlockSpec((1, seq, LANES), lambda b, c: (b, 0, c))
    return pl.pallas_call(
        _seqdft_kernel,
        grid=(batch, width // LANES),
        in_specs=[z_spec, z_spec,
                  _const_spec((DFT_N2, 2 * DFT_N1, 2 * DFT_N1)),
                  _const_spec((DFT_N2, 2 * DFT_N2))],
        out_specs=pl.BlockSpec((1, seq, LANES), lambda b, c: (b, 0, c)),
        out_shape=jax.ShapeDtypeStruct((batch, seq, width), F32),
        scratch_shapes=[pltpu.VMEM((DFT_N2 * 2 * DFT_N1, LANES), F32)],
        compiler_params=pltpu.CompilerParams(
            dimension_semantics=("parallel", "parallel"), vmem_limit_bytes=VMEM_LIMIT_BYTES),
        name="seqdft",
    )(zr3, zi3, m1, f2)


def _dft_tables(seq, group):
    assert seq == DFT_N1 * DFT_N2
    c = np.arange(group)
    ang = 2.0 * np.pi * ((c[:, None] * c[None, :]) % group) / group
    fc = np.concatenate([np.cos(ang), np.sin(ang)], axis=1)

    k1 = np.arange(DFT_N1)[None, :, None]
    n1 = np.arange(DFT_N1)[None, None, :]
    n2 = np.arange(DFT_N2)[:, None, None]
    th = 2.0 * np.pi * ((k1 * (DFT_N2 * n1 + n2)) % seq) / seq
    co, si = np.cos(th), np.sin(th)
    m1 = np.concatenate([np.concatenate([co, si], axis=2),
                         np.concatenate([-si, co], axis=2)], axis=1)

    k2 = np.arange(DFT_N2)
    ph = 2.0 * np.pi * ((k2[:, None] * k2[None, :]) % DFT_N2) / DFT_N2
    f2 = np.concatenate([np.cos(ph), np.sin(ph)], axis=1) / math.sqrt(seq * group)
    return tuple(jnp.asarray(t, dtype=F32).astype(BF16) for t in (fc, m1, f2))


def _merge_kernel(x_ref, att_ref, cv_ref, fr_ref, gates_ref,
                  cb_ref, lng_ref, lnb_ref, wpa_ref, wpb_ref, wpc_ref, wout_ref,
                  n2g_ref, wfi_ref, wfo_ref, o_ref):
    tm = ROW_BLOCK
    cv = cv_ref[...] + cb_ref[...]
    mu = jnp.mean(cv, axis=-1, keepdims=True)
    d = cv - mu
    var = jnp.mean(d * d, axis=-1, keepdims=True)
    cv = d * jax.lax.rsqrt(var + EPS) * lng_ref[...] + lnb_ref[...]
    sb = (cv * jax.nn.sigmoid(cv)).astype(BF16)

    y_a = jnp.dot(att_ref[...], wpa_ref[...], preferred_element_type=F32)
    y_b = jnp.dot(sb, wpb_ref[...], preferred_element_type=F32)
    y_c = jnp.dot(fr_ref[...].astype(BF16), wpc_ref[...], preferred_element_type=F32)
    merged = (gates_ref[:, 0:D_MODEL].astype(F32) * y_a
              + gates_ref[:, D_MODEL:2 * D_MODEL].astype(F32) * y_b
              + gates_ref[:, 2 * D_MODEL:].astype(F32) * y_c)
    x1 = x_ref[...] + jnp.dot(merged.astype(BF16), wout_ref[...], preferred_element_type=F32)

    h2 = x1 * jax.lax.rsqrt(jnp.mean(x1 * x1, axis=-1, keepdims=True) + EPS) * n2g_ref[...]
    h2b = h2.astype(BF16)
    acc = jnp.zeros((tm, D_MODEL), F32)
    for j in range(N_FF_CHUNKS):
        lo = j * FF_CHUNK
        gt = jnp.dot(h2b, wfi_ref[:, lo:lo + FF_CHUNK], preferred_element_type=F32)
        up = jnp.dot(h2b, wfi_ref[:, D_FF + lo:D_FF + lo + FF_CHUNK], preferred_element_type=F32)
        a = (gt * jax.nn.sigmoid(gt) * up).astype(BF16)
        acc = acc + jnp.dot(a, wfo_ref[j], preferred_element_type=F32)
    o_ref[...] = x1 + acc


def _merge_call(x2, att, cv, fr, gates, cb, lng, lnb, wpa, wpb, wpc, wout, n2g, wfi, wfo):
    rows = x2.shape[0]
    tm = ROW_BLOCK
    row_spec = lambda w: pl.BlockSpec((tm, w), lambda i: (i, 0))
    return pl.pallas_call(
        _merge_kernel,
        grid=(rows // tm,),
        in_specs=[
            row_spec(D_MODEL), row_spec(WIDTH), row_spec(WIDTH),
            row_spec(WIDTH), row_spec(3 * D_MODEL),
            _const_spec((1, WIDTH)), _const_spec((1, WIDTH)), _const_spec((1, WIDTH)),
            _const_spec((WIDTH, D_MODEL)), _const_spec((WIDTH, D_MODEL)), _const_spec((WIDTH, D_MODEL)),
            _const_spec((D_MODEL, D_MODEL)),
            _const_spec((1, D_MODEL)),
            _const_spec((D_MODEL, 2 * D_FF)),
            _const_spec((N_FF_CHUNKS, FF_CHUNK, D_MODEL)),
        ],
        out_specs=row_spec(D_MODEL),
        out_shape=jax.ShapeDtypeStruct((rows, D_MODEL), F32),
        compiler_params=pltpu.CompilerParams(
            dimension_semantics=("parallel",), vmem_limit_bytes=VMEM_LIMIT_BYTES),
        name="merge_ffn",
    )(x2, att, cv, fr, gates, cb, lng, lnb, wpa, wpb, wpc, wout, n2g, wfi, wfo)


def _rope_lane_tables(seq):
    pos = jnp.arange(seq, dtype=F32)
    inv = 1.0 / (ROPE_THETA ** (jnp.arange(0, ROPE_DIMS, 2, dtype=F32) / ROPE_DIMS))
    ang = pos[:, None] * inv[None, :]
    cos, sin = jnp.cos(ang), jnp.sin(ang)
    d = np.arange(LANES) % D_COMP
    f = d % ROPE_HALF
    rc = jnp.where(d[None, :] < ROPE_DIMS, cos[:, f], 1.0)
    rs_next = jnp.where(d[None, :] < ROPE_HALF, -sin[:, f], 0.0)
    rs_prev = jnp.where((d[None, :] >= ROPE_HALF) & (d[None, :] < ROPE_DIMS), sin[:, f], 0.0)
    return rc.astype(F32), rs_next.astype(F32), rs_prev.astype(F32)


def kernel(x, norm1_g, w_in, qnorm_g, knorm_g, lambda_q1, lambda_k1, lambda_q2, lambda_k2, subln_g,
           w_proj_a, conv_w, conv_b, conv_ln_g, conv_ln_b, w_proj_b, w_proj_c, w_gate, b_gate, w_out,
           norm2_g, w_ffn_in, w_ffn_out):
    batch, seq, d = x.shape
    depth = w_in.shape[0]
    rows = batch * seq
    rc, rsn, rsp = _rope_lane_tables(seq)
    fc, m1, f2 = _dft_tables(seq, GROUP_C)
    gidx = np.arange(MXU_DIM) // D_COMP
    gmean = jnp.asarray((gidx[:, None] == gidx[None, :]) / D_COMP, dtype=F32).astype(BF16)

    x2 = x.reshape(rows, d)
    for l in range(depth):
        lam_init = 0.8 - 0.6 * math.exp(-0.3 * l)
        row = lambda a: a[l].reshape(1, -1)
        q, kt, v, glu, zr, zi, gates = _proj_call(
            x2, row(norm1_g), w_in[l].astype(BF16), w_gate[l].astype(BF16), row(b_gate),
            jnp.tile(qnorm_g[l], WIDTH // D_COMP).reshape(1, WIDTH),
            jnp.tile(knorm_g[l], WIDTH // D_COMP).reshape(1, WIDTH),
            rc, rsn, rsp, gmean, fc, batch, seq)
        score_bound = (SCORE_BOUND_MARGIN * LOG2_E * math.sqrt(D_COMP) * jnp.max(jnp.abs(qnorm_g[l]))
                       * jnp.max(jnp.abs(knorm_g[l]))).reshape(1).astype(F32)
        cw = jnp.concatenate([conv_w[l], jnp.zeros((1, WIDTH), F32)], axis=0)
        att, cv = _attn_call(
            score_bound, q.reshape(batch, seq, WIDTH), kt, v.reshape(batch, seq, WIDTH),
            row(lambda_q1), row(lambda_k1), row(lambda_q2), row(lambda_k2),
            jnp.tile(subln_g[l], LANES // D_V).reshape(1, LANES),
            glu.reshape(batch, seq, WIDTH), cw, lam_init)
        fr = _seqdft_call(zr.reshape(batch, seq, WIDTH), zi.reshape(batch, seq, WIDTH), m1, f2)
        wfi = w_ffn_in[l].astype(BF16)
        wfo = w_ffn_out[l].astype(BF16).reshape(N_FF_CHUNKS, FF_CHUNK, d)
        x2 = _merge_call(
            x2, att.reshape(rows, WIDTH), cv.reshape(rows, WIDTH), fr.reshape(rows, WIDTH), gates,
            row(conv_b), row(conv_ln_g), row(conv_ln_b),
            w_proj_a[l].astype(BF16), w_proj_b[l].astype(BF16), w_proj_c[l].astype(BF16),
            w_out[l].astype(BF16), row(norm2_g), wfi, wfo)
    return x2.reshape(batch, seq, d)
```

```python
import functools
import math

import numpy as np
import jax
import jax.numpy as jnp
from jax.experimental import pallas as pl
from jax.experimental.pallas import tpu as pltpu

D_MODEL = 1024
N_HEADS = 8
D_COMP = 32
D_V = 64
WIDTH = 512
ROPE_DIMS = 8
ROPE_HALF = 4
ROPE_THETA = 500000.0
CONV_WIDTH = 31
CONV_PAD = 15
GROUP_C = 128
D_FF = 2816
FF_CHUNK = 256
N_FF_CHUNKS = D_FF // FF_CHUNK
EPS = 1e-6

LANES = 128
SUBLANES = 8
MXU_DIM = 256
VMEM_LIMIT_BYTES = 56 * 1024 * 1024

DFT_N1 = 128
DFT_N2 = 64

ROW_BLOCK = 512
HALO_ROWS = 16
CONV_ROW_CHUNK = 64
Q_BLOCK = 1024
K_SUB = 1024
K_CHUNK = 2048
K_AUG = 64
K_F8 = 256
SAFE_SCORE_BOUND = 40.0
SCORE_BOUND_MARGIN = 1.02
LOG2_E = math.log2(math.e)

BF16 = jnp.bfloat16
F8 = jnp.float8_e4m3fn
F32 = jnp.float32


def _const_spec(shape):
    nd = len(shape)
    return pl.BlockSpec(shape, lambda *_: (0,) * nd, pipeline_mode=pl.Buffered(1))


def _split_dot(a_f32, b_bf16):
    hi = a_f32.astype(BF16)
    lo = (a_f32 - hi.astype(F32)).astype(BF16)
    return (jnp.dot(hi, b_bf16, preferred_element_type=F32)
            + jnp.dot(lo, b_bf16, preferred_element_type=F32))


def _norm_rope(t, gain, group_mean, rc, rs_next, rs_prev):
    sq = t * t
    ms = jnp.concatenate([_split_dot(sq[:, j:j + MXU_DIM], group_mean)
                          for j in range(0, WIDTH, MXU_DIM)], axis=1)
    tn = t * jax.lax.rsqrt(ms + EPS) * gain
    outs = []
    for j in range(WIDTH // LANES):
        c = tn[:, j * LANES:(j + 1) * LANES]
        outs.append(c * rc + pltpu.roll(c, LANES - ROPE_HALF, 1) * rs_next
                    + pltpu.roll(c, ROPE_HALF, 1) * rs_prev)
    return jnp.concatenate(outs, axis=1)


def _proj_kernel(x_ref, n1g_ref, win_ref, wgate_ref, bgate_ref, qg_ref, kg_ref,
                 rc_ref, rsn_ref, rsp_ref, gmean_ref, fc_ref,
                 q_ref, kt_ref, v_ref, glu_ref, zr_ref, zi_ref, gates_ref):
    x = x_ref[...]
    h = x * jax.lax.rsqrt(jnp.mean(x * x, axis=-1, keepdims=True) + EPS) * n1g_ref[...]
    hb = h.astype(BF16)

    def seg(lo, width):
        return jnp.dot(hb, win_ref[:, lo:lo + width], preferred_element_type=F32)

    rc, rsn, rsp = rc_ref[...], rsn_ref[...], rsp_ref[...]
    gmean = gmean_ref[...]
    q = _norm_rope(seg(0, WIDTH), qg_ref[...], gmean, rc, rsn, rsp)
    q_ref[...] = (q * (D_COMP ** -0.5 * LOG2_E)).astype(BF16)
    k = _norm_rope(seg(WIDTH, WIDTH), kg_ref[...], gmean, rc, rsn, rsp)
    kt_ref[0] = k.T.astype(BF16)
    v_ref[...] = seg(2 * WIDTH, WIDTH).astype(BF16)

    ga = seg(3 * WIDTH, WIDTH)
    gb = seg(4 * WIDTH, WIDTH)
    glu_ref[...] = ga * jax.nn.sigmoid(gb)

    f = seg(5 * WIDTH, WIDTH)
    fc = fc_ref[...]
    for j in range(WIDTH // GROUP_C):
        ab = jnp.dot(f[:, j * GROUP_C:(j + 1) * GROUP_C].astype(BF16), fc,
                     preferred_element_type=F32)
        zr_ref[:, j * GROUP_C:(j + 1) * GROUP_C] = ab[:, :GROUP_C]
        zi_ref[:, j * GROUP_C:(j + 1) * GROUP_C] = -ab[:, GROUP_C:]

    for j in range(3):
        lo = j * D_MODEL
        gt = jnp.dot(hb, wgate_ref[:, lo:lo + D_MODEL], preferred_element_type=F32)
        gates_ref[:, lo:lo + D_MODEL] = jax.nn.sigmoid(gt + bgate_ref[:, lo:lo + D_MODEL]).astype(BF16)


def _proj_call(x2, n1g, win, wgate, bgate, qg, kg, rc, rsn, rsp, gmean, fc, batch, seq):
    rows = x2.shape[0]
    tm = ROW_BLOCK
    blocks_per_seq = seq // tm
    row_spec = lambda w: pl.BlockSpec((tm, w), lambda i: (i, 0))
    tab_spec = pl.BlockSpec((tm, LANES), lambda i: (i % blocks_per_seq, 0))
    in_width = win.shape[1]
    return pl.pallas_call(
        _proj_kernel,
        grid=(rows // tm,),
        in_specs=[
            row_spec(D_MODEL),
            _const_spec((1, D_MODEL)),
            _const_spec((D_MODEL, in_width)),
            _const_spec((D_MODEL, 3 * D_MODEL)),
            _const_spec((1, 3 * D_MODEL)),
            _const_spec((1, WIDTH)),
            _const_spec((1, WIDTH)),
            tab_spec, tab_spec, tab_spec,
            _const_spec((MXU_DIM, MXU_DIM)),
            _const_spec((GROUP_C, 2 * GROUP_C)),
        ],
        out_specs=[
            row_spec(WIDTH),
            pl.BlockSpec((1, WIDTH, tm), lambda i: (i // blocks_per_seq, 0, i % blocks_per_seq)),
            row_spec(WIDTH),
            row_spec(WIDTH),
            row_spec(WIDTH),
            row_spec(WIDTH),
            row_spec(3 * D_MODEL),
        ],
        out_shape=[
            jax.ShapeDtypeStruct((rows, WIDTH), BF16),
            jax.ShapeDtypeStruct((batch, WIDTH, seq), BF16),
            jax.ShapeDtypeStruct((rows, WIDTH), BF16),
            jax.ShapeDtypeStruct((rows, WIDTH), F32),
            jax.ShapeDtypeStruct((rows, WIDTH), F32),
            jax.ShapeDtypeStruct((rows, WIDTH), F32),
            jax.ShapeDtypeStruct((rows, 3 * D_MODEL), BF16),
        ],
        compiler_params=pltpu.CompilerParams(
            dimension_semantics=("parallel",), vmem_limit_bytes=VMEM_LIMIT_BYTES),
        name="proj",
    )(x2, n1g, win, wgate, bgate, qg, kg, rc, rsn, rsp, gmean, fc)


def _split3(x):
    a = x.astype(F8).astype(F32)
    r = x - a
    b = (r * 16.0).astype(F8).astype(F32)
    r = r - b * 0.0625
    c = (r * 256.0).astype(F8).astype(F32)
    return a, b, c


def _attn_kernel(bound_ref, q_ref, kt_ref, v_ref, lq1_ref, lk1_ref, lq2_ref, lk2_ref, sg_ref,
                 glu_ref, gprev_ref, gnext_ref, cw_ref,
                 o_ref, cv_ref, vext_ref, k8_ref, kaug_ref, col_ref, acc_ref, win_ref,
                 *, lam_init, seq):
    tq = Q_BLOCK
    trips = seq // K_CHUNK
    conv_units_per_trip = tq // (CONV_ROW_CHUNK * 2 * trips)
    assert conv_units_per_trip * CONV_ROW_CHUNK * 2 * trips == tq
    assert HALO_ROWS - CONV_PAD == 1
    blk = pl.program_id(2)
    win_ref[0:HALO_ROWS, :] = jnp.where(blk == 0, 0.0, gprev_ref[0])
    win_ref[HALO_ROWS:HALO_ROWS + tq, :] = glu_ref[0]
    win_ref[HALO_ROWS + tq:, :] = jnp.where(blk == pl.num_programs(2) - 1, 0.0, gnext_ref[0])

    def conv_unit(unit):
        r0 = pl.multiple_of(unit * CONV_ROW_CHUNK, CONV_ROW_CHUNK)
        acc = None
        for b in range(SUBLANES):
            part = None
            for a in range(-(-(CONV_WIDTH + 1) // SUBLANES)):
                t = SUBLANES * a + b - 1
                if 0 <= t < CONV_WIDTH:
                    rows = pl.ds(r0 + SUBLANES * a, CONV_ROW_CHUNK + SUBLANES)
                    term = cw_ref[t:t + 1, :] * win_ref[rows, :]
                    part = term if part is None else part + term
            shifted = part[b:b + CONV_ROW_CHUNK]
            acc = shifted if acc is None else acc + shifted
        cv_ref[0, pl.ds(r0, CONV_ROW_CHUNK), :] = acc

    lane = jax.lax.broadcasted_iota(jnp.int32, (1, LANES), 1)
    first_head = lane < D_V
    comp_lanes = lane < D_COMP
    use_row_max = bound_ref[0] > SAFE_SCORE_BOUND * LOG2_E
    use_bound = jnp.logical_not(use_row_max)

    @pl.when(pl.program_id(2) == 0)
    def _prepare_keys_values():
        v = v_ref[0]
        one = jnp.ones_like(v)
        vext_ref[0] = jnp.where(first_head, v, one)
        vext_ref[1] = jnp.where(first_head, one, v)

        @pl.when(use_row_max)
        def _bf16_keys():
            row = jax.lax.broadcasted_iota(jnp.int32, (K_AUG - D_COMP, seq), 0)
            extra = jnp.where(row == 0, 1.0, 0.0).astype(BF16)
            for c in range(4):
                kaug_ref[c, 0:D_COMP, :] = kt_ref[0, c * D_COMP:(c + 1) * D_COMP, :]
                kaug_ref[c, D_COMP:, :] = extra

        @pl.when(use_bound)
        def _fp8_keys():
            nb = LANES // D_COMP
            zero = jnp.zeros((D_COMP, seq), F8)
            for c in range(4):
                a, b, c3 = _split3(kt_ref[0, c * D_COMP:(c + 1) * D_COMP, :].astype(F32))
                lo = [t.astype(F8) for t in (a, a * 0.25, b * 0.25, b * 0.0625)]
                hi = [(a * 0.0625).astype(F8), (c3 * 0.0625).astype(F8), zero, zero]
                for p in range(nb):
                    k8_ref[c, p * D_COMP:(p + 1) * D_COMP, :] = lo[(p - c) % nb]
                    k8_ref[c, (nb + p) * D_COMP:(nb + p + 1) * D_COMP, :] = hi[(p - c) % nb]

    lam = (jnp.exp(jnp.sum(lq1_ref[...] * lk1_ref[...], axis=1, keepdims=True))
           - jnp.exp(jnp.sum(lq2_ref[...] * lk2_ref[...], axis=1, keepdims=True))
           + lam_init)

    def component(c):
        qf = q_ref[0].astype(F32)
        qc = qf if c == 0 else pltpu.roll(qf, LANES - c * D_COMP, 1)
        return jnp.where(comp_lanes, qc, 0.0)

    def fp8_query_maker():
        a, b, c3 = _split3(q_ref[0].astype(F32))
        nb = LANES // D_COMP
        block = lane // D_COMP
        lo_terms = [a] + [pltpu.roll(t, j * D_COMP, 1)
                          for j, t in ((1, b * 0.25), (2, a * 0.25), (3, b * 0.0625))]
        hi_terms = [c3 * 0.0625, pltpu.roll(a * 0.0625, D_COMP, 1)]

        def query_fp8(c):
            lo = lo_terms[nb - 1]
            for j in range(nb - 2, -1, -1):
                lo = jnp.where(block == (c + j) % nb, lo_terms[j], lo)
            hi = jnp.where(block == c, hi_terms[0],
                           jnp.where(block == (c + 1) % nb, hi_terms[1], 0.0))
            return jnp.concatenate([lo, hi], axis=1).astype(F8)
        return query_fp8

    def query_bf16(c):
        qc = component(c)
        q0 = qc[:, :K_AUG].astype(BF16)

        def mx(j, m):
            start = pl.multiple_of(j * K_SUB, K_SUB)
            s = jnp.dot(q0, kaug_ref[c, :, pl.ds(start, K_SUB)], preferred_element_type=F32)
            return jnp.maximum(m, jnp.max(s, axis=1, keepdims=True))
        col_ref[...] = -jax.lax.fori_loop(0, seq // K_SUB, mx, jnp.full((tq, 1), -jnp.inf, F32))
        qa = jnp.where(lane == D_COMP, col_ref[...], qc)
        return qa[:, :K_AUG].astype(BF16)

    def softmax_pv(head, qa1, qa2, k_ref, exp2_fn):
        acc_ref[...] = jnp.zeros_like(acc_ref)

        def body(j, carry):
            pv = None
            tiles = K_CHUNK // K_SUB
            for u in range(tiles):
                start = pl.multiple_of(j * K_CHUNK + u * K_SUB, K_SUB)
                s1 = jnp.dot(qa1, k_ref[2 * head, :, pl.ds(start, K_SUB)],
                             preferred_element_type=F32)
                s2 = jnp.dot(qa2, k_ref[2 * head + 1, :, pl.ds(start, K_SUB)],
                             preferred_element_type=F32)
                p = jnp.concatenate([exp2_fn(s1), exp2_fn(s2)], axis=0).astype(BF16)
                d = jnp.dot(p, vext_ref[head, pl.ds(start, K_SUB), :], preferred_element_type=F32)
                pv = d if pv is None else pv + d
                for w in range(u * conv_units_per_trip // tiles, (u + 1) * conv_units_per_trip // tiles):
                    conv_unit((head * trips + j) * conv_units_per_trip + w)
            acc_ref[...] += pv
            return carry

        jax.lax.fori_loop(0, trips, body, 0)
        return acc_ref[...]

    def attend(make_query, k_ref, exp2_fn):
        outs = []
        for head in range(2):
            valid = first_head if head == 0 else jnp.logical_not(first_head)
            acc = softmax_pv(head, make_query(2 * head), make_query(2 * head + 1), k_ref, exp2_fn)
            a = acc / pltpu.roll(acc, D_V, 1)
            o = a[:tq] - lam * a[tq:]
            ss = jnp.sum(jnp.where(valid, o * o, 0.0), axis=1, keepdims=True) * (1.0 / D_V)
            outs.append(o * jax.lax.rsqrt(ss + EPS))
        y = jnp.where(first_head, outs[0], outs[1]) * sg_ref[...] * (1.0 - lam_init)
        o_ref[0] = y.astype(BF16)

    @pl.when(use_bound)
    def _fp8_scores():
        attend(fp8_query_maker(), k8_ref, lambda s: jnp.exp2(s.astype(BF16)))

    @pl.when(use_row_max)
    def _bf16_scores():
        attend(query_bf16, kaug_ref, jnp.exp2)


def _attn_call(score_bound, q3, kt, v3, lq1, lk1, lq2, lk2, sg, glu3, cw, lam_init):
    batch, seq, _ = q3.shape
    pairs = N_HEADS // 2
    vec = _const_spec((1, D_COMP))
    halo_per_block = Q_BLOCK // HALO_ROWS
    n_halo_blocks = seq // HALO_ROWS
    row_block = pl.BlockSpec((1, Q_BLOCK, LANES), lambda b, h, i: (b, i, h))
    return pl.pallas_call(
        functools.partial(_attn_kernel, lam_init=lam_init, seq=seq),
        grid=(batch, pairs, seq // Q_BLOCK),
        in_specs=[
            pl.BlockSpec(memory_space=pltpu.SMEM),
            row_block,
            pl.BlockSpec((1, LANES, seq), lambda b, h, i: (b, h, 0)),
            pl.BlockSpec((1, seq, LANES), lambda b, h, i: (b, 0, h)),
            vec, vec, vec, vec,
            _const_spec((1, LANES)),
            row_block,
            pl.BlockSpec((1, HALO_ROWS, LANES),
                         lambda b, h, i: (b, jnp.maximum(i * halo_per_block - 1, 0), h)),
            pl.BlockSpec((1, HALO_ROWS, LANES),
                         lambda b, h, i: (b, jnp.minimum((i + 1) * halo_per_block, n_halo_blocks - 1), h)),
            pl.BlockSpec((CONV_WIDTH + 1, LANES), lambda b, h, i: (0, h)),
        ],
        out_specs=[row_block, row_block],
        out_shape=[jax.ShapeDtypeStruct((batch, seq, WIDTH), BF16),
                   jax.ShapeDtypeStruct((batch, seq, WIDTH), F32)],
        scratch_shapes=[
            pltpu.VMEM((2, seq, LANES), BF16),
            pltpu.VMEM((4, K_F8, seq), F8),
            pltpu.VMEM((4, K_AUG, seq), BF16),
            pltpu.VMEM((Q_BLOCK, 1), F32),
            pltpu.VMEM((2 * Q_BLOCK, LANES), F32),
            pltpu.VMEM((Q_BLOCK + 2 * HALO_ROWS, LANES), F32),
        ],
        compiler_params=pltpu.CompilerParams(
            dimension_semantics=("parallel", "parallel", "arbitrary"),
            vmem_limit_bytes=VMEM_LIMIT_BYTES),
        name="attn",
    )(score_bound, q3, kt, v3, lq1, lk1, lq2, lk2, sg, glu3, glu3, glu3, cw)


def _seqdft_kernel(zr_ref, zi_ref, m1_ref, f2_ref, o_ref, t_ref):
    def stage1(n2, carry):
        rows = pl.ds(n2, DFT_N1, stride=DFT_N2)
        z = jnp.concatenate([zr_ref[0, rows, :], zi_ref[0, rows, :]], axis=0).astype(BF16)
        t = jnp.dot(m1_ref[n2], z, preferred_element_type=F32)
        t_ref[pl.ds(pl.multiple_of(n2 * 2 * DFT_N1, 2 * DFT_N1), 2 * DFT_N1), :] = t
        return carry

    jax.lax.fori_loop(0, DFT_N2, stage1, 0, unroll=8)
    f2 = f2_ref[...]

    def stage2(k1, carry):
        tr = t_ref[pl.ds(k1, DFT_N2, stride=2 * DFT_N1), :]
        ti = t_ref[pl.ds(k1 + DFT_N1, DFT_N2, stride=2 * DFT_N1), :]
        t = jnp.concatenate([tr, ti], axis=0).astype(BF16)
        y = jnp.dot(f2, t, preferred_element_type=F32)
        o_ref[0, pl.ds(k1, DFT_N2, stride=DFT_N1), :] = y
        return carry

    jax.lax.fori_loop(0, DFT_N1, stage2, 0, unroll=8)


def _seqdft_call(zr3, zi3, m1, f2):
    batch, seq, width = zr3.shape
    z_spec = pl.BlockSpec((1, seq, LANES), lambda b, c: (b, 0, c))
    return pl.pallas_call(
        _seqdft_kernel,
        grid=(batch, width // LANES),
        in_specs=[z_spec, z_spec,
                  _const_spec((DFT_N2, 2 * DFT_N1, 2 * DFT_N1)),
                  _const_spec((DFT_N2, 2 * DFT_N2))],
        out_specs=pl.BlockSpec((1, seq, LANES), lambda b, c: (b, 0, c)),
        out_shape=jax.ShapeDtypeStruct((batch, seq, width), F32),
        scratch_shapes=[pltpu.VMEM((DFT_N2 * 2 * DFT_N1, LANES), F32)],
        compiler_params=pltpu.CompilerParams(
            dimension_semantics=("parallel", "parallel"), vmem_limit_bytes=VMEM_LIMIT_BYTES),
        name="seqdft",
    )(zr3, zi3, m1, f2)


def _dft_tables(seq, group):
    assert seq == DFT_N1 * DFT_N2
    c = np.arange(group)
    ang = 2.0 * np.pi * ((c[:, None] * c[None, :]) % group) / group
    fc = np.concatenate([np.cos(ang), np.sin(ang)], axis=1)

    k1 = np.arange(DFT_N1)[None, :, None]
    n1 = np.arange(DFT_N1)[None, None, :]
    n2 = np.arange(DFT_N2)[:, None, None]
    th = 2.0 * np.pi * ((k1 * (DFT_N2 * n1 + n2)) % seq) / seq
    co, si = np.cos(th), np.sin(th)
    m1 = np.concatenate([np.concatenate([co, si], axis=2),
                         np.concatenate([-si, co], axis=2)], axis=1)

    k2 = np.arange(DFT_N2)
    ph = 2.0 * np.pi * ((k2[:, None] * k2[None, :]) % DFT_N2) / DFT_N2
    f2 = np.concatenate([np.cos(ph), np.sin(ph)], axis=1) / math.sqrt(seq * group)
    return tuple(jnp.asarray(t, dtype=F32).astype(BF16) for t in (fc, m1, f2))


def _merge_kernel(x_ref, att_ref, cv_ref, fr_ref, gates_ref,
                  cb_ref, lng_ref, lnb_ref, wpa_ref, wpb_ref, wpc_ref, wout_ref,
                  n2g_ref, wfi_ref, wfo_ref, o_ref):
    tm = ROW_BLOCK
    cv = cv_ref[...] + cb_ref[...]
    mu = jnp.mean(cv, axis=-1, keepdims=True)
    d = cv - mu
    var = jnp.mean(d * d, axis=-1, keepdims=True)
    cv = d * jax.lax.rsqrt(var + EPS) * lng_ref[...] + lnb_ref[...]
    sb = (cv * jax.nn.sigmoid(cv)).astype(BF16)

    y_a = jnp.dot(att_ref[...], wpa_ref[...], preferred_element_type=F32)
    y_b = jnp.dot(sb, wpb_ref[...], preferred_element_type=F32)
    y_c = jnp.dot(fr_ref[...].astype(BF16), wpc_ref[...], preferred_element_type=F32)
    merged = (gates_ref[:, 0:D_MODEL].astype(F32) * y_a
              + gates_ref[:, D_MODEL:2 * D_MODEL].astype(F32) * y_b
              + gates_ref[:, 2 * D_MODEL:].astype(F32) * y_c)
    x1 = x_ref[...] + jnp.dot(merged.astype(BF16), wout_ref[...], preferred_element_type=F32)

    h2 = x1 * jax.lax.rsqrt(jnp.mean(x1 * x1, axis=-1, keepdims=True) + EPS) * n2g_ref[...]
    h2b = h2.astype(BF16)
    acc = jnp.zeros((tm, D_MODEL), F32)
    for j in range(N_FF_CHUNKS):
        lo = j * FF_CHUNK
        gt = jnp.dot(h2b, wfi_ref[:, lo:lo + FF_CHUNK], preferred_element_type=F32)
        up = jnp.dot(h2b, wfi_ref[:, D_FF + lo:D_FF + lo + FF_CHUNK], preferred_element_type=F32)
        a = (gt * jax.nn.sigmoid(gt) * up).astype(BF16)
        acc = acc + jnp.dot(a, wfo_ref[j], preferred_element_type=F32)
    o_ref[...] = x1 + acc


def _merge_call(x2, att, cv, fr, gates, cb, lng, lnb, wpa, wpb, wpc, wout, n2g, wfi, wfo):
    rows = x2.shape[0]
    tm = ROW_BLOCK
    row_spec = lambda w: pl.BlockSpec((tm, w), lambda i: (i, 0))
    return pl.pallas_call(
        _merge_kernel,
        grid=(rows // tm,),
        in_specs=[
            row_spec(D_MODEL), row_spec(WIDTH), row_spec(WIDTH),
            row_spec(WIDTH), row_spec(3 * D_MODEL),
            _const_spec((1, WIDTH)), _const_spec((1, WIDTH)), _const_spec((1, WIDTH)),
            _const_spec((WIDTH, D_MODEL)), _const_spec((WIDTH, D_MODEL)), _const_spec((WIDTH, D_MODEL)),
            _const_spec((D_MODEL, D_MODEL)),
            _const_spec((1, D_MODEL)),
            _const_spec((D_MODEL, 2 * D_FF)),
            _const_spec((N_FF_CHUNKS, FF_CHUNK, D_MODEL)),
        ],
        out_specs=row_spec(D_MODEL),
        out_shape=jax.ShapeDtypeStruct((rows, D_MODEL), F32),
        compiler_params=pltpu.CompilerParams(
            dimension_semantics=("parallel",), vmem_limit_bytes=VMEM_LIMIT_BYTES),
        name="merge_ffn",
    )(x2, att, cv, fr, gates, cb, lng, lnb, wpa, wpb, wpc, wout, n2g, wfi, wfo)


def _rope_lane_tables(seq):
    pos = jnp.arange(seq, dtype=F32)
    inv = 1.0 / (ROPE_THETA ** (jnp.arange(0, ROPE_DIMS, 2, dtype=F32) / ROPE_DIMS))
    ang = pos[:, None] * inv[None, :]
    cos, sin = jnp.cos(ang), jnp.sin(ang)
    d = np.arange(LANES) % D_COMP
    f = d % ROPE_HALF
    rc = jnp.where(d[None, :] < ROPE_DIMS, cos[:, f], 1.0)
    rs_next = jnp.where(d[None, :] < ROPE_HALF, -sin[:, f], 0.0)
    rs_prev = jnp.where((d[None, :] >= ROPE_HALF) & (d[None, :] < ROPE_DIMS), sin[:, f], 0.0)
    return rc.astype(F32), rs_next.astype(F32), rs_prev.astype(F32)


def kernel(x, norm1_g, w_in, qnorm_g, knorm_g, lambda_q1, lambda_k1, lambda_q2, lambda_k2, subln_g,
           w_proj_a, conv_w, conv_b, conv_ln_g, conv_ln_b, w_proj_b, w_proj_c, w_gate, b_gate, w_out,
           norm2_g, w_ffn_in, w_ffn_out):
    batch, seq, d = x.shape
    depth = w_in.shape[0]
    rows = batch * seq
    rc, rsn, rsp = _rope_lane_tables(seq)
    fc, m1, f2 = _dft_tables(seq, GROUP_C)
    gidx = np.arange(MXU_DIM) // D_COMP
    gmean = jnp.asarray((gidx[:, None] == gidx[None, :]) / D_COMP, dtype=F32).astype(BF16)

    x2 = x.reshape(rows, d)
    for l in range(depth):
        lam_init = 0.8 - 0.6 * math.exp(-0.3 * l)
        row = lambda a: a[l].reshape(1, -1)
        q, kt, v, glu, zr, zi, gates = _proj_call(
            x2, row(norm1_g), w_in[l].astype(BF16), w_gate[l].astype(BF16), row(b_gate),
            jnp.tile(qnorm_g[l], WIDTH // D_COMP).reshape(1, WIDTH),
            jnp.tile(knorm_g[l], WIDTH // D_COMP).reshape(1, WIDTH),
            rc, rsn, rsp, gmean, fc, batch, seq)
        score_bound = (SCORE_BOUND_MARGIN * LOG2_E * math.sqrt(D_COMP) * jnp.max(jnp.abs(qnorm_g[l]))
                       * jnp.max(jnp.abs(knorm_g[l]))).reshape(1).astype(F32)
        cw = jnp.concatenate([conv_w[l], jnp.zeros((1, WIDTH), F32)], axis=0)
        att, cv = _attn_call(
            score_bound, q.reshape(batch, seq, WIDTH), kt, v.reshape(batch, seq, WIDTH),
            row(lambda_q1), row(lambda_k1), row(lambda_q2), row(lambda_k2),
            jnp.tile(subln_g[l], LANES // D_V).reshape(1, LANES),
            glu.reshape(batch, seq, WIDTH), cw, lam_init)
        fr = _seqdft_call(zr.reshape(batch, seq, WIDTH), zi.reshape(batch, seq, WIDTH), m1, f2)
        wfi = w_ffn_in[l].astype(BF16)
        wfo = w_ffn_out[l].astype(BF16).reshape(N_FF_CHUNKS, FF_CHUNK, d)
        x2 = _merge_call(
            x2, att.reshape(rows, WIDTH), cv.reshape(rows, WIDTH), fr.reshape(rows, WIDTH), gates,
            row(conv_b), row(conv_ln_g), row(conv_ln_b),
            w_proj_a[l].astype(BF16), w_proj_b[l].astype(BF16), w_proj_c[l].astype(BF16),
            w_out[l].astype(BF16), row(norm2_g), wfi, wfo)
    return x2.reshape(batch, seq, d)
```
